```python
import math
import jax
import jax.numpy as jnp
from jax import lax
import numpy as np

D_MODEL = 1024
BATCH = 16
SEQ = 2048
DEPTH = 2

HEAD_DIM = 64
MIX_WIDTH = D_MODEL
N_HEADS_A = (MIX_WIDTH // 2) // (2 * HEAD_DIM)
DV_A = 2 * HEAD_DIM
N_HEADS_B = (MIX_WIDTH // 2) // HEAD_DIM
DILATED_PATTERNS = ((128, 1), (512, 4), (2048, 16))
N_HEADS_C = (MIX_WIDTH // 2) // HEAD_DIM
MOBA_BLOCK = 256
MOBA_TOPK = 3
MOBA_Q_CHUNK = 16
N_HEADS_D = (MIX_WIDTH // 2) // HEAD_DIM
FORGET_BIAS_INIT = 4.0
Q_BLOCK = 128
NUM_BUCKETS = 32
MAX_DISTANCE = 128
N_BIAS_HEADS = max(N_HEADS_A + N_HEADS_B, N_HEADS_C)
D_FF = -(-8 * D_MODEL // (3 * 256)) * 256
LN_EPS = 1e-5
DEEPNORM_ALPHA = (2 * DEPTH) ** 0.25
DEEPNORM_BETA = (8 * DEPTH) ** -0.25
ATTN_SCALE = HEAD_DIM ** -0.5

AB_SIZES = [N_HEADS_A * 2 * HEAD_DIM, N_HEADS_A * 2 * HEAD_DIM, N_HEADS_A * DV_A,
            N_HEADS_B * HEAD_DIM, N_HEADS_B * HEAD_DIM, N_HEADS_B * HEAD_DIM]
CD_SIZES = [N_HEADS_C * HEAD_DIM] * 3 + [N_HEADS_D * HEAD_DIM] * 3 + [N_HEADS_D]
AB_SPLITS = [int(s) for s in np.cumsum(AB_SIZES)[:-1]]
CD_SPLITS = [int(s) for s in np.cumsum(CD_SIZES)[:-1]]
IN_AB = sum(AB_SIZES)
IN_CD = sum(CD_SIZES)

kernel_name = "hybrid_diff_dilated_moba_fox_block"

f32 = jnp.float32


def t5_bucket(dist):
    n = jnp.maximum(dist, 0)
    max_exact = NUM_BUCKETS // 2
    nf = jnp.maximum(n, 1).astype(f32)
    large = max_exact + (jnp.log(nf / max_exact) / math.log(MAX_DISTANCE / max_exact)
                         * (NUM_BUCKETS - max_exact)).astype(jnp.int32)
    large = jnp.minimum(large, NUM_BUCKETS - 1)
    return jnp.where(n < max_exact, n, large)


def t5_bias(table_cols, dist):
    return jnp.moveaxis(table_cols[t5_bucket(dist)], -1, 0).astype(f32)


def layer_norm(x, g, b):
    xf = x.astype(f32)
    mu = xf.mean(-1, keepdims=True)
    var = jnp.square(xf - mu).mean(-1, keepdims=True)
    return ((xf - mu) * lax.rsqrt(var + LN_EPS) * g + b).astype(x.dtype)


def causal_block_attention(q, k, v, map_w, bias_fn):
    B, H, M, T, _ = q.shape
    k_pos = jnp.arange(T)

    def block(q0):
        qb = lax.dynamic_slice_in_dim(q, q0, Q_BLOCK, axis=3)
        q_pos = q0 + jnp.arange(Q_BLOCK)
        s = jnp.einsum('bhmqd,bhmkd->bhmqk', qb, k).astype(f32) + bias_fn(q_pos)
        s = jnp.where(k_pos[None, :] <= q_pos[:, None], s, -jnp.inf)
        p = jnp.einsum('bhmqk,m->bhqk', jax.nn.softmax(s, axis=-1), map_w)
        return jnp.einsum('bhqk,bhkd->bhqd', p.astype(v.dtype), v)

    out = lax.map(block, jnp.arange(T // Q_BLOCK) * Q_BLOCK)
    return jnp.moveaxis(out, 0, 2).reshape(B, H, T, v.shape[-1])


def dilated_attention(q, k, v, table_cols):
    B, H, T, dh = q.shape
    outs, lses = [], []
    for window, dil in DILATED_PATTERNS:
        W = window // dil
        L = T // dil
        nb = -(-L // W)
        pad = nb * W - L

        def strided(a):
            a = a.reshape(B, H, L, dil, a.shape[-1]).swapaxes(2, 3)
            return jnp.pad(a, ((0, 0), (0, 0), (0, 0), (0, pad), (0, 0)))

        def with_prev(a):
            ab = a.reshape(B, H, dil, nb, W, a.shape[-1])
            prev = jnp.pad(ab, ((0, 0), (0, 0), (0, 0), (1, 0), (0, 0), (0, 0)))[:, :, :, :-1]
            return jnp.concatenate([prev, ab], axis=4)

        qb = strided(q).reshape(B, H, dil, nb, W, dh)
        kb = with_prev(strided(k))
        vb = with_prev(strided(v))
        u_q = jnp.arange(nb)[:, None] * W + jnp.arange(W)[None, :]
        u_k = jnp.arange(nb)[:, None] * W + jnp.arange(-W, W)[None, :]
        du = u_q[:, :, None] - u_k[:, None, :]
        valid = (du >= 0) & (du <= W) & (u_k[:, None, :] >= 0)
        bias = t5_bias(table_cols, du * dil)
        s = jnp.einsum('bhrnqd,bhrnkd->bhrnqk', qb, kb).astype(f32) + bias[None, :, None]
        s = jnp.where(valid, s, -jnp.inf)
        m = s.max(-1, keepdims=True)
        p = jnp.exp(s - m)
        den = p.sum(-1)
        o = jnp.einsum('bhrnqk,bhrnkd->bhrnqd', p.astype(vb.dtype), vb) / den[..., None]
        o = o.reshape(B, H, dil, nb * W, dh)[:, :, :, :L].swapaxes(2, 3).reshape(B, H, T, dh)
        lse = (m[..., 0] + jnp.log(den)).reshape(B, H, dil, nb * W)[..., :L]
        outs.append(o)
        lses.append(lse.swapaxes(2, 3).reshape(B, H, T))
    w = jax.nn.softmax(jnp.stack(lses), axis=0)
    return jnp.sum(w[..., None] * jnp.stack(outs), axis=0).astype(q.dtype)


def moba_attention(q, k, v, table_cols):
    B, H, T, dh = q.shape
    n_blk = -(-T // MOBA_BLOCK)
    pad = n_blk * MOBA_BLOCK - T
    kp = jnp.pad(k, ((0, 0), (0, 0), (0, pad), (0, 0)))
    vp = jnp.pad(v, ((0, 0), (0, 0), (0, pad), (0, 0)))
    k_blk = kp.reshape(B, H, n_blk, MOBA_BLOCK, dh)
    v_blk = vp.reshape(B, H, n_blk, MOBA_BLOCK, dh)
    k_mean = k_blk.mean(axis=3)
    top = min(MOBA_TOPK, n_blk)
    b_idx = jnp.arange(B)[:, None, None, None]
    h_idx = jnp.arange(H)[None, :, None, None]
    bias_flat = table_cols.T.reshape(-1).astype(f32)
    in_blk = jnp.arange(MOBA_BLOCK)

    def chunk(q0):
        qc = lax.dynamic_slice_in_dim(q, q0, MOBA_Q_CHUNK, axis=2)
        q_pos = q0 + jnp.arange(MOBA_Q_CHUNK)
        own = q0 // MOBA_BLOCK
        gate = jnp.einsum('bhcd,bhnd->bhcn', qc, k_mean).astype(f32)
        gate = jnp.where(jnp.arange(n_blk) < own, gate, -jnp.inf)
        top_s, top_i = lax.top_k(gate, top)
        sel_ok = jnp.isfinite(top_s)
        k_sel = k_blk[b_idx, h_idx, top_i]
        v_sel = v_blk[b_idx, h_idx, top_i]
        sel_pos = top_i[..., None] * MOBA_BLOCK + in_blk
        sel_bucket = t5_bucket(q_pos[:, None, None] - sel_pos)
        s_sel = (jnp.einsum('bhcd,bhcnkd->bhcnk', qc, k_sel).astype(f32)
                 + bias_flat[h_idx[..., None] * NUM_BUCKETS + sel_bucket])
        s_sel = jnp.where(sel_ok[..., None], s_sel, -jnp.inf)
        own_start = own * MOBA_BLOCK
        k_own = lax.dynamic_slice_in_dim(kp, own_start, MOBA_BLOCK, axis=2)
        v_own = lax.dynamic_slice_in_dim(vp, own_start, MOBA_BLOCK, axis=2)
        own_dist = q_pos[:, None] - (own_start + in_blk)[None, :]
        s_own = (jnp.einsum('bhcd,bhkd->bhck', qc, k_own).astype(f32)
                 + t5_bias(table_cols, own_dist)[None])
        s_own = jnp.where(own_dist >= 0, s_own, -jnp.inf)
        s = jnp.concatenate([s_sel.reshape(B, H, MOBA_Q_CHUNK, top * MOBA_BLOCK), s_own], axis=-1)
        p = jax.nn.softmax(s, axis=-1).astype(v.dtype)
        p_sel = p[..., :top * MOBA_BLOCK].reshape(B, H, MOBA_Q_CHUNK, top, MOBA_BLOCK)
        p_own = p[..., top * MOBA_BLOCK:]
        return (jnp.einsum('bhcnk,bhcnkd->bhcd', p_sel, v_sel)
                + jnp.einsum('bhck,bhkd->bhcd', p_own, v_own))

    out = lax.map(chunk, jnp.arange(T // MOBA_Q_CHUNK) * MOBA_Q_CHUNK)
    return jnp.moveaxis(out, 0, 2).reshape(B, H, T, dh)


def split_heads(a, dh):
    B, T, _ = a.shape
    return a.reshape(B, T, -1, dh).transpose(0, 2, 1, 3)


def merge_heads(a):
    B, H, T, dh = a.shape
    return a.transpose(0, 2, 1, 3).reshape(B, T, H * dh)


def mixer_ab(h, w_in, w_o, lam, subln_g, table, lambda_init):
    B, T, _ = h.shape
    qa, ka, va, qb, kb, vb = jnp.split(h @ w_in, AB_SPLITS, axis=-1)
    qa = qa.reshape(B, T, N_HEADS_A, 2, HEAD_DIM).transpose(0, 2, 3, 1, 4) * ATTN_SCALE
    ka = ka.reshape(B, T, N_HEADS_A, 2, HEAD_DIM).transpose(0, 2, 3, 1, 4)
    va = split_heads(va, DV_A)
    lamf = lam.astype(f32)
    lam_full = (jnp.exp(jnp.sum(lamf[0] * lamf[1])) - jnp.exp(jnp.sum(lamf[2] * lamf[3]))
                + lambda_init)
    map_w = jnp.stack([jnp.ones((), f32), -lam_full])
    table_a = table[:, :N_HEADS_A]
    k_pos = jnp.arange(T)
    bias_a = lambda q_pos: t5_bias(table_a, q_pos[:, None] - k_pos[None, :])[None, :, None]
    oa = causal_block_attention(qa, ka, va, map_w, bias_a).astype(f32)
    oa = oa * lax.rsqrt(jnp.mean(jnp.square(oa), -1, keepdims=True) + LN_EPS)
    oa = (oa * subln_g * (1.0 - lambda_init)).astype(h.dtype)
    ob = dilated_attention(split_heads(qb, HEAD_DIM) * ATTN_SCALE, split_heads(kb, HEAD_DIM),
                           split_heads(vb, HEAD_DIM), table[:, N_HEADS_A:N_HEADS_A + N_HEADS_B])
    return jnp.concatenate([merge_heads(oa), merge_heads(ob)], axis=-1) @ w_o


def mixer_cd(h, w_in, w_o, forget_b, table):
    qc, kc, vc, qd, kd, vd, fd = jnp.split(h @ w_in, CD_SPLITS, axis=-1)
    oc = moba_attention(split_heads(qc, HEAD_DIM) * ATTN_SCALE, split_heads(kc, HEAD_DIM),
                        split_heads(vc, HEAD_DIM), table[:, :N_HEADS_C])
    log_f = jax.nn.log_sigmoid((fd + forget_b).astype(f32)).transpose(0, 2, 1)
    cum = jnp.cumsum(log_f, axis=-1)
    fox_bias = lambda q_pos: (cum[:, :, q_pos, None] - cum[:, :, None, :])[:, :, None]
    od = causal_block_attention((split_heads(qd, HEAD_DIM) * ATTN_SCALE)[:, :, None],
                                split_heads(kd, HEAD_DIM)[:, :, None],
                                split_heads(vd, HEAD_DIM), jnp.ones((1,), f32), fox_bias)
    return jnp.concatenate([merge_heads(oc), merge_heads(od)], axis=-1) @ w_o


def swiglu(h, w_in, w_out):
    g, u = jnp.split(h @ w_in, 2, axis=-1)
    return (jax.nn.silu(g) * u) @ w_out


def setup_inputs(seed: int = 0) -> dict:
    key = jax.random.key(seed)
    ks = jax.random.split(key, 15)
    n_even = (DEPTH + 1) // 2
    n_odd = DEPTH // 2
    nrm = lambda k, shape, std: jax.random.normal(k, shape, f32) * std
    return {
        "x": nrm(ks[0], (BATCH, SEQ, D_MODEL), 1.0),
        "c": nrm(ks[1], (BATCH, D_MODEL), 1.0),
        "rel_bias": nrm(ks[2], (NUM_BUCKETS, N_BIAS_HEADS), 0.5),
        "w_ada": nrm(ks[3], (DEPTH, D_MODEL, 6 * D_MODEL), 0.1 * D_MODEL ** -0.5),
        "b_ada": nrm(ks[4], (DEPTH, 6 * D_MODEL), 0.01),
        "ln_g": 1.0 + nrm(ks[5], (DEPTH, 2, D_MODEL), 0.01),
        "ln_b": nrm(ks[6], (DEPTH, 2, D_MODEL), 0.01),
        "w_in_ab": nrm(ks[7], (n_even, D_MODEL, IN_AB), D_MODEL ** -0.5),
        "diff_lambda": nrm(ks[8], (n_even, 4, HEAD_DIM), 0.1),
        "diff_subln_g": 1.0 + nrm(ks[9], (n_even, DV_A), 0.01),
        "w_in_cd": nrm(ks[10], (n_odd, D_MODEL, IN_CD), D_MODEL ** -0.5),
        "forget_b": FORGET_BIAS_INIT + nrm(ks[11], (n_odd, N_HEADS_D), 0.5),
        "w_o": nrm(ks[12], (DEPTH, MIX_WIDTH, D_MODEL), DEEPNORM_BETA * MIX_WIDTH ** -0.5),
        "w_ffn_in": nrm(ks[13], (DEPTH, D_MODEL, 2 * D_FF), D_MODEL ** -0.5),
        "w_ffn_out": nrm(ks[14], (DEPTH, D_FF, D_MODEL), DEEPNORM_BETA * D_FF ** -0.5),
    }


def reference(x, c, rel_bias, w_ada, b_ada, ln_g, ln_b, w_in_ab, diff_lambda, diff_subln_g,
              w_in_cd, forget_b, w_o, w_ffn_in, w_ffn_out):
    for l in range(DEPTH):
        ada = jax.nn.silu(c) @ w_ada[l] + b_ada[l]
        sh1, sc1, g1, sh2, sc2, g2 = [a[:, None, :] for a in jnp.split(ada, 6, axis=-1)]
        h = x * (1.0 + sc1) + sh1
        if l % 2 == 0:
            i = l // 2
            lambda_init = 0.8 - 0.6 * math.exp(-0.3 * l)
            y = mixer_ab(h, w_in_ab[i], w_o[l], diff_lambda[i], diff_subln_g[i], rel_bias,
                         lambda_init)
        else:
            i = l // 2
            y = mixer_cd(h, w_in_cd[i], w_o[l], forget_b[i], rel_bias)
        x = layer_norm(DEEPNORM_ALPHA * x + (1.0 + g1) * y, ln_g[l, 0], ln_b[l, 0])
        h = x * (1.0 + sc2) + sh2
        y = swiglu(h, w_ffn_in[l], w_ffn_out[l])
        x = layer_norm(DEEPNORM_ALPHA * x + (1.0 + g2) * y, ln_g[l, 1], ln_b[l, 1])
    return x
```

```python
import functools
import math

import numpy as np
import jax
import jax.numpy as jnp
from jax import lax
from jax.experimental import pallas as pl
from jax.experimental.pallas import tpu as pltpu

f32 = jnp.float32
bf16 = jnp.bfloat16

D_MODEL = 1024
SEQ = 2048
DEPTH = 2
HEAD_DIM = 64
N_HEADS_A = 4
DV_A = 2 * HEAD_DIM
N_HEADS_B = 8
DILATED_PATTERNS = ((128, 1), (512, 4), (2048, 16))
DIL_BLOCK = 128
N_HEADS_C = 8
MOBA_BLOCK = 256
MOBA_TOPK = 3
N_MOBA_BLOCKS = SEQ // MOBA_BLOCK
N_HEADS_D = 8
NUM_BUCKETS = 32
MAX_DISTANCE = 128
D_FF = 2816
LN_EPS = 1e-5
DEEPNORM_ALPHA = (2 * DEPTH) ** 0.25
ATTN_SCALE = HEAD_DIM ** -0.5
HALF = D_MODEL // 2
IN_MAIN = 3 * D_MODEL

LANES = 128
ATT_TILE = 256
N_ATT_TILES = SEQ // ATT_TILE
FF_CHUNK = D_FF // 2
ROW_TILE = 512
VMEM_LIMIT = 56 * 1024 * 1024

NEG_INF = float("-inf")


def _t5_bucket_starts():
    n = np.arange(0, 4 * SEQ, dtype=np.int64)
    max_exact = NUM_BUCKETS // 2
    nf = np.maximum(n, 1).astype(np.float32)
    val = (np.log(nf / np.float32(max_exact)) / np.float32(math.log(MAX_DISTANCE / max_exact))
           * np.float32(NUM_BUCKETS - max_exact))
    large = np.minimum(max_exact + val.astype(np.int32), NUM_BUCKETS - 1)
    bucket = np.where(n < max_exact, n, large)
    assert np.all(np.diff(bucket) >= 0)
    return tuple(int(np.argmax(bucket >= b)) for b in range(NUM_BUCKETS))


T5_STARTS = _t5_bucket_starts()
assert T5_STARTS[-1] <= ATT_TILE + 1


def _t5_bias(dist, tbl_ref, col):
    out = jnp.full(dist.shape, tbl_ref[0, col], f32)
    for b in range(1, NUM_BUCKETS):
        out = jnp.where(dist >= T5_STARTS[b], tbl_ref[b, col], out)
    return out


def _dot_nt(a, b):
    return lax.dot_general(a, b, (((1,), (1,)), ((), ())), preferred_element_type=f32)


def _dot(a, b):
    return jnp.dot(a, b, preferred_element_type=f32)


def _flash_step(carry, s, v):
    m, l, acc = carry
    m_new = jnp.maximum(m, jnp.max(s, axis=1, keepdims=True))
    alpha = jnp.exp(m - m_new)
    p = jnp.exp(s - m_new)
    l = alpha * l + jnp.sum(p, axis=1, keepdims=True)
    acc = alpha * acc + _dot(p.astype(bf16), v)
    return m_new, l, acc


def _flash_init(dv):
    return (jnp.full((ATT_TILE, 1), NEG_INF, f32), jnp.zeros((ATT_TILE, 1), f32),
            jnp.zeros((ATT_TILE, dv), f32))


def _tile_iotas(n):
    row = lax.broadcasted_iota(jnp.int32, (n, n), 0)
    col = lax.broadcasted_iota(jnp.int32, (n, n), 1)
    return row, col


def _layer_norm(x, g, b):
    mu = jnp.mean(x, axis=1, keepdims=True)
    xc = x - mu
    var = jnp.mean(xc * xc, axis=1, keepdims=True)
    return xc * lax.rsqrt(var + LN_EPS) * g + b


def _ada_kernel(c_ref, w_ref, b_ref, o_ref):
    c = c_ref[...]
    cs = c * jax.nn.sigmoid(c)
    o_ref[0] = _dot(cs.astype(bf16), w_ref[0].astype(bf16)) + b_ref[0]


def _ada(c, w_ada, b_ada):
    B = c.shape[0]
    tn = 1536
    return pl.pallas_call(
        _ada_kernel,
        grid=(DEPTH, 6 * D_MODEL // tn),
        in_specs=[
            pl.BlockSpec((B, D_MODEL), lambda l, j: (0, 0)),
            pl.BlockSpec((1, D_MODEL, tn), lambda l, j: (l, 0, j)),
            pl.BlockSpec((1, 1, tn), lambda l, j: (l, 0, j)),
        ],
        out_specs=pl.BlockSpec((1, B, tn), lambda l, j: (l, 0, j)),
        out_shape=jax.ShapeDtypeStruct((DEPTH, B, 6 * D_MODEL), f32),
        compiler_params=pltpu.CompilerParams(
            dimension_semantics=("arbitrary", "arbitrary"), vmem_limit_bytes=VMEM_LIMIT),
        name="ada",
    )(c, w_ada, b_ada.reshape(DEPTH, 1, 6 * D_MODEL))


def _proj_kernel(x_ref, sc_ref, sh_ref, w_ref, o_ref):
    h = (x_ref[0] * (1.0 + sc_ref[0]) + sh_ref[0]).astype(bf16)
    o_ref[0] = _dot(h, w_ref[...]).astype(bf16)


def _proj_gate_kernel(x_ref, sc_ref, sh_ref, w_ref, wf_ref, o_ref, f_ref):
    h = (x_ref[0] * (1.0 + sc_ref[0]) + sh_ref[0]).astype(bf16)
    o_ref[0] = _dot(h, w_ref[...]).astype(bf16)
    f_ref[0] = _dot(h, wf_ref[...])


def _resident(shape):
    zeros = (0,) * len(shape)
    return pl.BlockSpec(shape, lambda *_: zeros, pipeline_mode=pl.Buffered(1))


def _row_spec(width):
    return pl.BlockSpec((1, ROW_TILE, width), lambda b, i: (b, i, 0))


def _batch_vec_spec():
    return pl.BlockSpec((1, 1, D_MODEL), lambda b, i: (b, 0, 0))


def _in_proj(x, sc, sh, w, w_gate=None):
    B = x.shape[0]
    in_specs = [_row_spec(D_MODEL), _batch_vec_spec(), _batch_vec_spec(), _resident(w.shape)]
    out_specs = [_row_spec(IN_MAIN)]
    out_shape = [jax.ShapeDtypeStruct((B, SEQ, IN_MAIN), bf16)]
    args = [x, sc, sh, w]
    kern = _proj_kernel
    if w_gate is not None:
        in_specs.append(_resident(w_gate.shape))
        out_specs.append(_row_spec(LANES))
        out_shape.append(jax.ShapeDtypeStruct((B, SEQ, LANES), f32))
        args.append(w_gate)
        kern = _proj_gate_kernel
    res = pl.pallas_call(
        kern,
        grid=(B, SEQ // ROW_TILE),
        in_specs=in_specs,
        out_specs=out_specs,
        out_shape=out_shape,
        compiler_params=pltpu.CompilerParams(
            dimension_semantics=("arbitrary", "arbitrary"), vmem_limit_bytes=VMEM_LIMIT),
        name="in_proj",
    )(*args)
    return res if w_gate is not None else res[0]


def _diff_kernel(tbl_ref, lam_ref, g_ref, q_ref, k_ref, v_ref, o_ref, bd_ref, bs_ref, *,
                 lambda_init):
    h = pl.program_id(0)
    row, col = _tile_iotas(ATT_TILE)

    @pl.when(pl.program_id(1) == 0)
    def _():
        bd_ref[...] = _t5_bias(row - col, tbl_ref, h)
        bs_ref[...] = _t5_bias(row - col + ATT_TILE, tbl_ref, h)

    far_bias = tbl_ref[NUM_BUCKETS - 1, h]
    lm = lam_ref[...]
    lam_full = (jnp.exp(jnp.sum(lm[0:1] * lm[1:2], axis=1, keepdims=True))
                - jnp.exp(jnp.sum(lm[2:3] * lm[3:4], axis=1, keepdims=True)) + lambda_init)
    causal = row >= col

    def q_body(i, _):
        r0 = pl.multiple_of(i * ATT_TILE, ATT_TILE)
        rp = pl.multiple_of(jnp.maximum(i - 1, 0) * ATT_TILE, ATT_TILE)
        maps = []
        for m in range(2):
            sl = slice(m * HEAD_DIM, (m + 1) * HEAD_DIM)
            q = q_ref[0, pl.ds(r0, ATT_TILE), sl]
            s = _dot_nt(q, k_ref[0, pl.ds(r0, ATT_TILE), sl]) + bd_ref[...]
            s = jnp.where(causal, s, NEG_INF)
            carry = _flash_step(_flash_init(DV_A), s, v_ref[0, pl.ds(r0, ATT_TILE), :])
            s = _dot_nt(q, k_ref[0, pl.ds(rp, ATT_TILE), sl]) + bs_ref[...]
            s = jnp.where(i > 0, s, NEG_INF)
            carry = _flash_step(carry, s, v_ref[0, pl.ds(rp, ATT_TILE), :])

            def kv_body(j, c, q=q, sl=sl):
                c0 = pl.multiple_of(j * ATT_TILE, ATT_TILE)
                s = _dot_nt(q, k_ref[0, pl.ds(c0, ATT_TILE), sl]) + far_bias
                return _flash_step(c, s, v_ref[0, pl.ds(c0, ATT_TILE), :])

            _, l, acc = lax.fori_loop(0, jnp.maximum(i - 1, 0), kv_body, carry)
            maps.append(acc / l)
        o = maps[0] - lam_full * maps[1]
        o = o * lax.rsqrt(jnp.mean(o * o, axis=1, keepdims=True) + LN_EPS)
        o_ref[0, pl.ds(r0, ATT_TILE), :] = (o * g_ref[...] * (1.0 - lambda_init)).astype(bf16)
        return 0

    lax.fori_loop(0, N_ATT_TILES, q_body, 0)


def _smem_spec():
    return pl.BlockSpec(memory_space=pltpu.SMEM)


def _head_spec(first_block):
    return pl.BlockSpec((1, SEQ, LANES), lambda g, b: (b, 0, first_block + g))


def _diff_attention(qkv, table, lam, subln_g, lambda_init):
    B = qkv.shape[0]
    nblk = HALF // LANES
    return pl.pallas_call(
        functools.partial(_diff_kernel, lambda_init=lambda_init),
        grid=(N_HEADS_A, B),
        in_specs=[
            _smem_spec(),
            pl.BlockSpec((4, HEAD_DIM), lambda g, b: (0, 0)),
            pl.BlockSpec((1, DV_A), lambda g, b: (0, 0)),
            _head_spec(0), _head_spec(nblk), _head_spec(2 * nblk),
        ],
        out_specs=_head_spec(0),
        out_shape=jax.ShapeDtypeStruct((B, SEQ, HALF), bf16),
        scratch_shapes=[pltpu.VMEM((ATT_TILE, ATT_TILE), f32), pltpu.VMEM((ATT_TILE, ATT_TILE), f32)],
        compiler_params=pltpu.CompilerParams(
            dimension_semantics=("arbitrary", "arbitrary"), vmem_limit_bytes=VMEM_LIMIT),
        name="diff_attention",
    )(table, lam, subln_g.reshape(1, DV_A), qkv, qkv, qkv)


def _dil_kernel(tbl_ref, q_ref, k_ref, v_ref, o_ref, qf, kf, vf, bown, bprev, osc, lsc):
    hp = pl.program_id(0)
    W = DIL_BLOCK
    row, col = _tile_iotas(W)

    @pl.when(pl.program_id(1) == 0)
    def _():
        for p, (_, dil) in enumerate(DILATED_PATTERNS):
            for hh in range(2):
                head = N_HEADS_A + 2 * hp + hh
                bown[p, hh] = _t5_bias((row - col) * dil, tbl_ref, head)
                bprev[p, hh] = _t5_bias((row - col + W) * dil, tbl_ref, head)

    qf[...] = q_ref[0].astype(f32)
    kf[...] = k_ref[0].astype(f32)
    vf[...] = v_ref[0].astype(f32)
    own_ok = row >= col
    prev_ok = col >= row

    for p, (window, dil) in enumerate(DILATED_PATTERNS):
        assert window // dil == W
        nb = SEQ // dil // W

        def blk_body(blk, _, p=p, dil=dil, nb=nb):
            r = blk // nb
            n = blk % nb
            start = r + n * (W * dil)
            rows = pl.ds(start, W, stride=dil)
            qb = qf[rows, :].astype(bf16)
            ko = kf[rows, :].astype(bf16)
            vo = vf[rows, :].astype(bf16)
            if nb > 1:
                prow = pl.ds(jnp.maximum(start - W * dil, r), W, stride=dil)
                kp = kf[prow, :].astype(bf16)
                vp = vf[prow, :].astype(bf16)
            outs, lses = [], []
            for hh in range(2):
                sl = slice(hh * HEAD_DIM, (hh + 1) * HEAD_DIM)
                s_o = jnp.where(own_ok, _dot_nt(qb[:, sl], ko[:, sl]) + bown[p, hh], NEG_INF)
                m = jnp.max(s_o, axis=1, keepdims=True)
                if nb > 1:
                    s_p = jnp.where(prev_ok, _dot_nt(qb[:, sl], kp[:, sl]) + bprev[p, hh], NEG_INF)
                    s_p = jnp.where(n > 0, s_p, NEG_INF)
                    m = jnp.maximum(m, jnp.max(s_p, axis=1, keepdims=True))
                e_o = jnp.exp(s_o - m)
                den = jnp.sum(e_o, axis=1, keepdims=True)
                num = _dot(e_o.astype(bf16), vo[:, sl])
                if nb > 1:
                    e_p = jnp.exp(s_p - m)
                    den = den + jnp.sum(e_p, axis=1, keepdims=True)
                    num = num + _dot(e_p.astype(bf16), vp[:, sl])
                outs.append(num / den)
                lses.append(jnp.broadcast_to(m + jnp.log(den), (W, HEAD_DIM)))
            osc[p, rows, :] = jnp.concatenate(outs, axis=1)
            lsc[p, rows, :] = jnp.concatenate(lses, axis=1)
            return 0

        lax.fori_loop(0, SEQ // W, blk_body, 0)

    def merge_body(i, _):
        rows = pl.ds(pl.multiple_of(i * ATT_TILE, ATT_TILE), ATT_TILE)
        ls = [lsc[p, rows, :] for p in range(len(DILATED_PATTERNS))]
        mx = functools.reduce(jnp.maximum, ls)
        ws = [jnp.exp(l - mx) for l in ls]
        tot = functools.reduce(lambda a, b: a + b, ws)
        out = functools.reduce(lambda a, b: a + b,
                               [(w / tot) * osc[p, rows, :] for p, w in enumerate(ws)])
        o_ref[0, rows, :] = out.astype(bf16)
        return 0

    lax.fori_loop(0, N_ATT_TILES, merge_body, 0)


def _dilated_attention(qkv, table):
    B = qkv.shape[0]
    nblk = HALF // LANES
    first = 3 * nblk
    npat = len(DILATED_PATTERNS)
    return pl.pallas_call(
        _dil_kernel,
        grid=(N_HEADS_B // 2, B),
        in_specs=[_smem_spec(), _head_spec(first), _head_spec(first + nblk),
                  _head_spec(first + 2 * nblk)],
        out_specs=_head_spec(0),
        out_shape=jax.ShapeDtypeStruct((B, SEQ, HALF), bf16),
        scratch_shapes=[
            pltpu.VMEM((SEQ, LANES), f32), pltpu.VMEM((SEQ, LANES), f32), pltpu.VMEM((SEQ, LANES), f32),
            pltpu.VMEM((npat, 2, DIL_BLOCK, DIL_BLOCK), f32),
            pltpu.VMEM((npat, 2, DIL_BLOCK, DIL_BLOCK), f32),
            pltpu.VMEM((npat, SEQ, LANES), f32), pltpu.VMEM((npat, SEQ, LANES), f32),
        ],
        compiler_params=pltpu.CompilerParams(
            dimension_semantics=("arbitrary", "arbitrary"), vmem_limit_bytes=VMEM_LIMIT),
        name="dilated_attention",
    )(table, qkv, qkv, qkv)


def _moba_kernel(tbl_ref, q_ref, k_ref, v_ref, o_ref, bd_ref, bs_ref):
    hp = pl.program_id(0)
    row, col = _tile_iotas(ATT_TILE)
    assert MOBA_BLOCK == ATT_TILE and LANES % N_MOBA_BLOCKS == 0

    @pl.when(pl.program_id(1) == 0)
    def _():
        for hh in range(2):
            bd_ref[hh] = _t5_bias(row - col, tbl_ref, 2 * hp + hh)
            bs_ref[hh] = _t5_bias(row - col + ATT_TILE, tbl_ref, 2 * hp + hh)

    causal = row >= col
    blk_of_lane = lax.broadcasted_iota(jnp.int32, (ATT_TILE, LANES), 1) % N_MOBA_BLOCKS

    for hh in range(2):
        sl = slice(hh * HEAD_DIM, (hh + 1) * HEAD_DIM)
        far_bias = tbl_ref[NUM_BUCKETS - 1, 2 * hp + hh]
        means = [jnp.sum(k_ref[0, n * MOBA_BLOCK:(n + 1) * MOBA_BLOCK, sl].astype(f32), axis=0,
                         keepdims=True) * (1.0 / MOBA_BLOCK) for n in range(N_MOBA_BLOCKS)]
        k_mean = jnp.concatenate(means * (LANES // N_MOBA_BLOCKS), axis=0).astype(bf16)

        def q_body(i, _, hh=hh, sl=sl, far_bias=far_bias, k_mean=k_mean):
            r0 = pl.multiple_of(i * ATT_TILE, ATT_TILE)
            rp = pl.multiple_of(jnp.maximum(i - 1, 0) * ATT_TILE, ATT_TILE)
            q = q_ref[0, pl.ds(r0, ATT_TILE), sl]
            past = blk_of_lane < i
            gate = jnp.where(past, _dot_nt(q, k_mean), NEG_INF)
            rank = jnp.zeros((ATT_TILE, LANES), jnp.int32)
            for s in range(1, N_MOBA_BLOCKS):
                other = pltpu.roll(gate, s, axis=1)
                wins_tie = blk_of_lane >= s
                beats = (other > gate) | ((other == gate) & wins_tie)
                rank = rank + beats.astype(jnp.int32)
            sel = jnp.where(past & (rank < MOBA_TOPK), 0.0, NEG_INF)

            def block_mask(j):
                return jnp.max(jnp.where(blk_of_lane == j, sel, NEG_INF), axis=1, keepdims=True)

            s = _dot_nt(q, k_ref[0, pl.ds(r0, ATT_TILE), sl]) + bd_ref[hh]
            s = jnp.where(causal, s, NEG_INF)
            carry = _flash_step(_flash_init(HEAD_DIM), s, v_ref[0, pl.ds(r0, ATT_TILE), sl])
            s = (_dot_nt(q, k_ref[0, pl.ds(rp, ATT_TILE), sl]) + bs_ref[hh]
                 + block_mask(jnp.maximum(i - 1, 0)))
            carry = _flash_step(carry, s, v_ref[0, pl.ds(rp, ATT_TILE), sl])

            def kv_body(j, c):
                c0 = pl.multiple_of(j * ATT_TILE, ATT_TILE)
                s = _dot_nt(q, k_ref[0, pl.ds(c0, ATT_TILE), sl]) + far_bias + block_mask(j)
                return _flash_step(c, s, v_ref[0, pl.ds(c0, ATT_TILE), sl])

            _, l, acc = lax.fori_loop(0, jnp.maximum(i - 1, 0), kv_body, carry)
            o_ref[0, pl.ds(r0, ATT_TILE), sl] = (acc / l).astype(bf16)
            return 0

        lax.fori_loop(0, N_ATT_TILES, q_body, 0)


def _moba_attention(qkv, table):
    B = qkv.shape[0]
    nblk = HALF // LANES
    return pl.pallas_call(
        _moba_kernel,
        grid=(N_HEADS_C // 2, B),
        in_specs=[_smem_spec(), _head_spec(0), _head_spec(nblk), _head_spec(2 * nblk)],
        out_specs=_head_spec(0),
        out_shape=jax.ShapeDtypeStruct((B, SEQ, HALF), bf16),
        scratch_shapes=[pltpu.VMEM((2, ATT_TILE, ATT_TILE), f32),
                        pltpu.VMEM((2, ATT_TILE, ATT_TILE), f32)],
        compiler_params=pltpu.CompilerParams(
            dimension_semantics=("arbitrary", "arbitrary"), vmem_limit_bytes=VMEM_LIMIT),
        name="moba_attention",
    )(table, qkv, qkv, qkv)


def _forget_cum_kernel(fd_ref, fb_ref, col_ref, row_ref):
    x = jax.nn.log_sigmoid(fd_ref[0] + fb_ref[...])
    t = lax.broadcasted_iota(jnp.int32, (SEQ, LANES), 0)
    shift = 1
    while shift < SEQ:
        x = x + jnp.where(t >= shift, pltpu.roll(x, shift, axis=0), 0.0)
        shift *= 2
    col_ref[0] = x
    for c in range(SEQ // LANES):
        blk = x[c * LANES:(c + 1) * LANES, :].T
        row_ref[0, :, c * LANES:(c + 1) * LANES] = blk[0:N_HEADS_D, :]


def _forget_cum(fd, forget_b):
    B = fd.shape[0]
    fb = jnp.zeros((1, LANES), f32).at[0, :N_HEADS_D].set(forget_b)
    return pl.pallas_call(
        _forget_cum_kernel,
        grid=(B,),
        in_specs=[pl.BlockSpec((1, SEQ, LANES), lambda b: (b, 0, 0)),
                  pl.BlockSpec((1, LANES), lambda b: (0, 0))],
        out_specs=[pl.BlockSpec((1, SEQ, LANES), lambda b: (b, 0, 0)),
                   pl.BlockSpec((1, N_HEADS_D, SEQ), lambda b: (b, 0, 0))],
        out_shape=[jax.ShapeDtypeStruct((B, SEQ, LANES), f32),
                   jax.ShapeDtypeStruct((B, N_HEADS_D, SEQ), f32)],
        compiler_params=pltpu.CompilerParams(
            dimension_semantics=("arbitrary",), vmem_limit_bytes=VMEM_LIMIT),
        name="forget_cum",
    )(fd, fb)


def _fox_kernel(ccol_ref, crow_ref, q_ref, k_ref, v_ref, o_ref, col_sc):
    hp = pl.program_id(0)
    row, col = _tile_iotas(ATT_TILE)
    causal = row >= col
    lane = lax.broadcasted_iota(jnp.int32, (SEQ, LANES), 1)

    for hh in range(2):
        sl = slice(hh * HEAD_DIM, (hh + 1) * HEAD_DIM)
        h = 2 * hp + hh
        col_sc[...] = jnp.sum(jnp.where(lane == h, ccol_ref[0], 0.0), axis=1, keepdims=True)

        def q_body(i, _, sl=sl, h=h):
            r0 = pl.multiple_of(i * ATT_TILE, ATT_TILE)
            q = q_ref[0, pl.ds(r0, ATT_TILE), sl]
            cq = col_sc[pl.ds(r0, ATT_TILE), :]

            def scores(c0):
                ck = crow_ref[0, pl.ds(h, 1), pl.ds(c0, ATT_TILE)]
                return _dot_nt(q, k_ref[0, pl.ds(c0, ATT_TILE), sl]) + (cq - ck)

            s = jnp.where(causal, scores(r0), NEG_INF)
            carry = _flash_step(_flash_init(HEAD_DIM), s, v_ref[0, pl.ds(r0, ATT_TILE), sl])

            def kv_body(j, c):
                c0 = pl.multiple_of(j * ATT_TILE, ATT_TILE)
                return _flash_step(c, scores(c0), v_ref[0, pl.ds(c0, ATT_TILE), sl])

            _, l, acc = lax.fori_loop(0, i, kv_body, carry)
            o_ref[0, pl.ds(r0, ATT_TILE), sl] = (acc / l).astype(bf16)
            return 0

        lax.fori_loop(0, N_ATT_TILES, q_body, 0)


def _fox_attention(qkv, cum_col, cum_row):
    B = qkv.shape[0]
    nblk = HALF // LANES
    first = 3 * nblk
    return pl.pallas_call(
        _fox_kernel,
        grid=(N_HEADS_D // 2, B),
        in_specs=[pl.BlockSpec((1, SEQ, LANES), lambda g, b: (b, 0, 0)),
                  pl.BlockSpec((1, N_HEADS_D, SEQ), lambda g, b: (b, 0, 0)),
                  _head_spec(first), _head_spec(first + nblk), _head_spec(first + 2 * nblk)],
        out_specs=_head_spec(0),
        out_shape=jax.ShapeDtypeStruct((B, SEQ, HALF), bf16),
        scratch_shapes=[pltpu.VMEM((SEQ, 1), f32)],
        compiler_params=pltpu.CompilerParams(
            dimension_semantics=("arbitrary", "arbitrary"), vmem_limit_bytes=VMEM_LIMIT),
        name="fox_attention",
    )(cum_col, cum_row, qkv, qkv, qkv)


def _post_kernel(o1_ref, o2_ref, x_ref, g1_ref, sc2_ref, sh2_ref, g2_ref, ln_ref,
                 wo_ref, wi_ref, wout_ref, out_ref):
    y = _dot(o1_ref[0], wo_ref[0:HALF, :]) + _dot(o2_ref[0], wo_ref[HALF:, :])
    ln = ln_ref[...]
    x1 = _layer_norm(DEEPNORM_ALPHA * x_ref[0] + (1.0 + g1_ref[0]) * y, ln[0:1], ln[1:2])
    h = (x1 * (1.0 + sc2_ref[0]) + sh2_ref[0]).astype(bf16)
    y2 = jnp.zeros((ROW_TILE, D_MODEL), f32)
    for c in range(D_FF // FF_CHUNK):
        g = _dot(h, wi_ref[:, c * FF_CHUNK:(c + 1) * FF_CHUNK])
        u = _dot(h, wi_ref[:, D_FF + c * FF_CHUNK:D_FF + (c + 1) * FF_CHUNK])
        a = (g * jax.nn.sigmoid(g) * u).astype(bf16)
        y2 = y2 + _dot(a, wout_ref[c * FF_CHUNK:(c + 1) * FF_CHUNK, :])
    out_ref[0] = _layer_norm(DEEPNORM_ALPHA * x1 + (1.0 + g2_ref[0]) * y2, ln[2:3], ln[3:4])


def _post(o1, o2, x, g1, sc2, sh2, g2, ln, w_o, w_ffn_in, w_ffn_out):
    B = x.shape[0]
    return pl.pallas_call(
        _post_kernel,
        grid=(B, SEQ // ROW_TILE),
        in_specs=[_row_spec(HALF), _row_spec(HALF), _row_spec(D_MODEL),
                  _batch_vec_spec(), _batch_vec_spec(), _batch_vec_spec(), _batch_vec_spec(),
                  pl.BlockSpec((4, D_MODEL), lambda b, i: (0, 0)),
                  _resident(w_o.shape), _resident(w_ffn_in.shape), _resident(w_ffn_out.shape)],
        out_specs=_row_spec(D_MODEL),
        out_shape=jax.ShapeDtypeStruct((B, SEQ, D_MODEL), f32),
        compiler_params=pltpu.CompilerParams(
            dimension_semantics=("arbitrary", "arbitrary"), vmem_limit_bytes=VMEM_LIMIT),
        name="out_proj_ffn",
    )(o1, o2, x, g1, sc2, sh2, g2, ln, w_o, w_ffn_in, w_ffn_out)


def _scaled_in_weights(w):
    scale = np.ones((IN_MAIN,), np.float32)
    scale[0:HALF] = ATTN_SCALE
    scale[3 * HALF:4 * HALF] = ATTN_SCALE
    return (w[:, :IN_MAIN] * scale).astype(bf16)


def kernel(x, c, rel_bias, w_ada, b_ada, ln_g, ln_b, w_in_ab, diff_lambda, diff_subln_g,
           w_in_cd, forget_b, w_o, w_ffn_in, w_ffn_out):
    B = x.shape[0]
    ada = _ada(c, w_ada, b_ada)
    for l in range(DEPTH):
        sh1, sc1, g1, sh2, sc2, g2 = [a.reshape(B, 1, D_MODEL) for a in jnp.split(ada[l], 6, axis=-1)]
        i = l // 2
        if l % 2 == 0:
            lambda_init = 0.8 - 0.6 * math.exp(-0.3 * l)
            qkv = _in_proj(x, sc1, sh1, _scaled_in_weights(w_in_ab[i]))
            o1 = _diff_attention(qkv, rel_bias, diff_lambda[i], diff_subln_g[i], lambda_init)
            o2 = _dilated_attention(qkv, rel_bias)
        else:
            w_gate = jnp.zeros((D_MODEL, LANES), f32).at[:, :N_HEADS_D].set(w_in_cd[i][:, IN_MAIN:])
            qkv, fd = _in_proj(x, sc1, sh1, _scaled_in_weights(w_in_cd[i]), w_gate.astype(bf16))
            cum_col, cum_row = _forget_cum(fd, forget_b[i])
            o1 = _moba_attention(qkv, rel_bias)
            o2 = _fox_attention(qkv, cum_col, cum_row)
        ln = jnp.stack([ln_g[l, 0], ln_b[l, 0], ln_g[l, 1], ln_b[l, 1]])
        x = _post(o1, o2, x, g1, sc2, sh2, g2, ln, w_o[l].astype(bf16),
                  w_ffn_in[l].astype(bf16), w_ffn_out[l].astype(bf16))
    return x
```

```python
import functools
import math

import numpy as np
import jax
import jax.numpy as jnp
from jax import lax
from jax.experimental import pallas as pl
from jax.experimental.pallas import tpu as pltpu

f32 = jnp.float32
bf16 = jnp.bfloat16

D_MODEL = 1024
SEQ = 2048
DEPTH = 2
HEAD_DIM = 64
N_HEADS_A = 4
DV_A = 2 * HEAD_DIM
N_HEADS_B = 8
DILATED_PATTERNS = ((128, 1), (512, 4), (2048, 16))
DIL_BLOCK = 128
N_HEADS_C = 8
MOBA_BLOCK = 256
MOBA_TOPK = 3
N_MOBA_BLOCKS = SEQ // MOBA_BLOCK
N_HEADS_D = 8
NUM_BUCKETS = 32
MAX_DISTANCE = 128
D_FF = 2816
LN_EPS = 1e-5
DEEPNORM_ALPHA = (2 * DEPTH) ** 0.25
ATTN_SCALE = HEAD_DIM ** -0.5
HALF = D_MODEL // 2
IN_MAIN = 3 * D_MODEL

LANES = 128
ATT_TILE = 256
KEY_CHUNK = 256
N_ATT_TILES = SEQ // ATT_TILE
PIPELINE_DEPTH = 4
FF_CHUNK = D_FF // 2
ROW_TILE = 512
VMEM_LIMIT = 56 * 1024 * 1024

NEG_INF = float("-inf")


def _t5_bucket_starts():
    n = np.arange(0, 4 * SEQ, dtype=np.int64)
    max_exact = NUM_BUCKETS // 2
    nf = np.maximum(n, 1).astype(np.float32)
    val = (np.log(nf / np.float32(max_exact)) / np.float32(math.log(MAX_DISTANCE / max_exact))
           * np.float32(NUM_BUCKETS - max_exact))
    large = np.minimum(max_exact + val.astype(np.int32), NUM_BUCKETS - 1)
    bucket = np.where(n < max_exact, n, large)
    assert np.all(np.diff(bucket) >= 0)
    return tuple(int(np.argmax(bucket >= b)) for b in range(NUM_BUCKETS))


T5_STARTS = _t5_bucket_starts()
assert T5_STARTS[-1] <= ATT_TILE + 1


def _t5_bias(dist, tbl_ref, col):
    out = jnp.full(dist.shape, tbl_ref[0, col], f32)
    for b in range(1, NUM_BUCKETS):
        out = jnp.where(dist >= T5_STARTS[b], tbl_ref[b, col], out)
    return out


def _dot_nt(a, b):
    return lax.dot_general(a, b, (((1,), (1,)), ((), ())), preferred_element_type=f32)


def _dot(a, b):
    return jnp.dot(a, b, preferred_element_type=f32)


def _software_pipeline(items, first_stage, second_stage):
    pending = [first_stage(it) for it in items[:PIPELINE_DEPTH]]
    for idx, it in enumerate(items):
        cur = pending.pop(0)
        if idx + PIPELINE_DEPTH < len(items):
            pending.append(first_stage(items[idx + PIPELINE_DEPTH]))
        second_stage(it, cur)


def _store_transposed(v_ref, vt_ref):
    for c in range(N_ATT_TILES):
        rows = slice(c * ATT_TILE, (c + 1) * ATT_TILE)
        vt_ref[:, rows] = v_ref[0, rows, :].astype(f32).T.astype(bf16)


def _causal_flash_t(streams, load_q, load_k, load_vt, score_mod, query_shift, emit):
    streams = list(streams)
    krow = lax.broadcasted_iota(jnp.int32, (KEY_CHUNK, ATT_TILE), 0)
    qcol = lax.broadcasted_iota(jnp.int32, (KEY_CHUNK, ATT_TILE), 1)
    per_tile = ATT_TILE // KEY_CHUNK
    state = {}

    def spans(work):
        _, i, c = work
        return (slice(i * ATT_TILE, (i + 1) * ATT_TILE), slice(c * KEY_CHUNK, (c + 1) * KEY_CHUNK))

    def first_stage(work):
        s, i, c = work
        queries, keys = spans(work)
        t = score_mod(s, queries, keys, _dot_nt(load_k(s, keys), load_q(s, queries)))
        if keys.stop - 1 > queries.start:
            t = jnp.where(krow + (keys.start - queries.start) <= qcol, t, NEG_INF)
        return t, jnp.max(t, axis=0, keepdims=True)

    def second_stage(work, staged):
        s, i, c = work
        queries, keys = spans(work)
        t, tmax = staged
        m, l, acc = state.get((s, i), (None, None, None))
        m_new = tmax if m is None else jnp.maximum(m, tmax)
        shift = query_shift(s, queries)
        if shift is None:
            p = jnp.exp(t - m_new)
        else:
            p = jnp.exp(t + (shift - (m_new + shift)))
        psum = jnp.sum(p, axis=0, keepdims=True)
        pv = _dot(load_vt(s, keys), p.astype(bf16))
        if m is None:
            l, acc = psum, pv
        else:
            alpha = jnp.exp(m - m_new)
            l, acc = alpha * l + psum, alpha * acc + pv
        state[(s, i)] = (m_new, l, acc)
        if (s, c) == last_of_tile[i]:
            emit(queries, [(state[(s2, i)][2], state[(s2, i)][1]) for s2 in streams])

    work, last_of_tile = [], {}
    for i in reversed(range(N_ATT_TILES)):
        first = i * per_tile
        order = [first] + [c for c in range((i + 1) * per_tile) if c != first]
        work += [(s, i, c) for c in order for s in streams]
        last_of_tile[i] = (streams[-1], order[-1])
    _software_pipeline(work, first_stage, second_stage)


def _tile_iotas(n):
    row = lax.broadcasted_iota(jnp.int32, (n, n), 0)
    col = lax.broadcasted_iota(jnp.int32, (n, n), 1)
    return row, col


def _layer_norm(x, g, b):
    mu = jnp.mean(x, axis=1, keepdims=True)
    xc = x - mu
    var = jnp.mean(xc * xc, axis=1, keepdims=True)
    return xc * lax.rsqrt(var + LN_EPS) * g + b


def _ada_kernel(c_ref, w_ref, b_ref, o_ref):
    c = c_ref[...]
    cs = c * jax.nn.sigmoid(c)
    o_ref[0] = _dot(cs.astype(bf16), w_ref[0].astype(bf16)) + b_ref[0]


def _ada(c, w_ada, b_ada):
    B = c.shape[0]
    tn = 1536
    return pl.pallas_call(
        _ada_kernel,
        grid=(DEPTH, 6 * D_MODEL // tn),
        in_specs=[
            pl.BlockSpec((B, D_MODEL), lambda l, j: (0, 0)),
            pl.BlockSpec((1, D_MODEL, tn), lambda l, j: (l, 0, j)),
            pl.BlockSpec((1, 1, tn), lambda l, j: (l, 0, j)),
        ],
        out_specs=pl.BlockSpec((1, B, tn), lambda l, j: (l, 0, j)),
        out_shape=jax.ShapeDtypeStruct((DEPTH, B, 6 * D_MODEL), f32),
        compiler_params=pltpu.CompilerParams(
            dimension_semantics=("arbitrary", "arbitrary"), vmem_limit_bytes=VMEM_LIMIT),
        name="ada",
    )(c, w_ada, b_ada.reshape(DEPTH, 1, 6 * D_MODEL))


def _proj_kernel(x_ref, sc_ref, sh_ref, w_ref, o_ref):
    h = (x_ref[0] * (1.0 + sc_ref[0]) + sh_ref[0]).astype(bf16)
    o_ref[0] = _dot(h, w_ref[...]).astype(bf16)


def _proj_gate_kernel(x_ref, sc_ref, sh_ref, w_ref, wf_ref, o_ref, f_ref):
    h = (x_ref[0] * (1.0 + sc_ref[0]) + sh_ref[0]).astype(bf16)
    o_ref[0] = _dot(h, w_ref[...]).astype(bf16)
    f_ref[0] = _dot(h, wf_ref[...])


def _resident(shape):
    zeros = (0,) * len(shape)
    return pl.BlockSpec(shape, lambda *_: zeros, pipeline_mode=pl.Buffered(1))


def _row_spec(width):
    return pl.BlockSpec((1, ROW_TILE, width), lambda b, i: (b, i, 0))


def _batch_vec_spec():
    return pl.BlockSpec((1, 1, D_MODEL), lambda b, i: (b, 0, 0))


def _in_proj(x, sc, sh, w, w_gate=None):
    B = x.shape[0]
    in_specs = [_row_spec(D_MODEL), _batch_vec_spec(), _batch_vec_spec(), _resident(w.shape)]
    out_specs = [_row_spec(IN_MAIN)]
    out_shape = [jax.ShapeDtypeStruct((B, SEQ, IN_MAIN), bf16)]
    args = [x, sc, sh, w]
    kern = _proj_kernel
    if w_gate is not None:
        in_specs.append(_resident(w_gate.shape))
        out_specs.append(_row_spec(LANES))
        out_shape.append(jax.ShapeDtypeStruct((B, SEQ, LANES), f32))
        args.append(w_gate)
        kern = _proj_gate_kernel
    res = pl.pallas_call(
        kern,
        grid=(B, SEQ // ROW_TILE),
        in_specs=in_specs,
        out_specs=out_specs,
        out_shape=out_shape,
        compiler_params=pltpu.CompilerParams(
            dimension_semantics=("arbitrary", "arbitrary"), vmem_limit_bytes=VMEM_LIMIT),
        name="in_proj",
    )(*args)
    return res if w_gate is not None else res[0]


def _near_bias_t(tbl_ref, head):
    kk = lax.broadcasted_iota(jnp.int32, (2 * ATT_TILE, ATT_TILE), 0)
    qq = lax.broadcasted_iota(jnp.int32, (2 * ATT_TILE, ATT_TILE), 1)
    return _t5_bias(qq - kk + ATT_TILE, tbl_ref, head) - tbl_ref[NUM_BUCKETS - 1, head]


def _add_near_bias(t, bias_ref, lead, queries, keys):
    rel = keys.start - (queries.start - ATT_TILE)
    if rel < 0:
        return t
    return t + bias_ref[lead + (slice(rel, rel + KEY_CHUNK), slice(None))]


def _smem_spec():
    return pl.BlockSpec(memory_space=pltpu.SMEM)


def _head_spec(first_block):
    return pl.BlockSpec((1, SEQ, LANES), lambda g, b: (b, 0, first_block + g))


def _diff_kernel(tbl_ref, lam_ref, g_ref, q_ref, k_ref, v_ref, o_ref, bias_sc, vt_sc, *,
                 lambda_init):
    h = pl.program_id(0)

    @pl.when(pl.program_id(1) == 0)
    def _():
        bias_sc[...] = _near_bias_t(tbl_ref, h)

    _store_transposed(v_ref, vt_sc)
    lm = lam_ref[...]
    lam_full = (jnp.exp(jnp.sum(lm[0:1] * lm[1:2], axis=1, keepdims=True))
                - jnp.exp(jnp.sum(lm[2:3] * lm[3:4], axis=1, keepdims=True)) + lambda_init)

    def load_q(m, queries):
        return q_ref[0, queries, m * HEAD_DIM:(m + 1) * HEAD_DIM]

    def load_k(m, keys):
        return k_ref[0, keys, m * HEAD_DIM:(m + 1) * HEAD_DIM]

    def load_vt(m, keys):
        return vt_sc[:, keys]

    def score_mod(m, queries, keys, t):
        return _add_near_bias(t, bias_sc, (), queries, keys)

    def emit(queries, results):
        (acc1, l1), (acc2, l2) = results
        o = (acc1 / l1 - lam_full * (acc2 / l2)).T
        o = o * lax.rsqrt(jnp.mean(o * o, axis=1, keepdims=True) + LN_EPS)
        o_ref[0, queries, :] = (o * g_ref[...] * (1.0 - lambda_init)).astype(bf16)

    _causal_flash_t(range(2), load_q, load_k, load_vt, score_mod, lambda m, queries: None, emit)


def _diff_attention(qkv, table, lam, subln_g, lambda_init):
    B = qkv.shape[0]
    nblk = HALF // LANES
    return pl.pallas_call(
        functools.partial(_diff_kernel, lambda_init=lambda_init),
        grid=(N_HEADS_A, B),
        in_specs=[
            _smem_spec(),
            pl.BlockSpec((4, HEAD_DIM), lambda g, b: (0, 0)),
            pl.BlockSpec((1, DV_A), lambda g, b: (0, 0)),
            _head_spec(0), _head_spec(nblk), _head_spec(2 * nblk),
        ],
        out_specs=_head_spec(0),
        out_shape=jax.ShapeDtypeStruct((B, SEQ, HALF), bf16),
        scratch_shapes=[pltpu.VMEM((2 * ATT_TILE, ATT_TILE), f32), pltpu.VMEM((LANES, SEQ), bf16)],
        compiler_params=pltpu.CompilerParams(
            dimension_semantics=("arbitrary", "arbitrary"), vmem_limit_bytes=VMEM_LIMIT),
        name="diff_attention",
    )(table, lam, subln_g.reshape(1, DV_A), qkv, qkv, qkv)


def _dil_kernel(tbl_ref, q_ref, k_ref, v_ref, o_ref, qf, kf, vf, bown, bprev, osc, lsc):
    hp = pl.program_id(0)
    W = DIL_BLOCK
    row, col = _tile_iotas(W)

    @pl.when(pl.program_id(1) == 0)
    def _():
        for p, (_, dil) in enumerate(DILATED_PATTERNS):
            for hh in range(2):
                head = N_HEADS_A + 2 * hp + hh
                bown[p, hh] = _t5_bias((row - col) * dil, tbl_ref, head)
                bprev[p, hh] = _t5_bias((row - col + W) * dil, tbl_ref, head)

    qf[...] = q_ref[0].astype(f32)
    kf[...] = k_ref[0].astype(f32)
    vf[...] = v_ref[0].astype(f32)
    own_ok = row >= col
    prev_ok = col >= row

    for p, (window, dil) in enumerate(DILATED_PATTERNS):
        assert window // dil == W
        nb = SEQ // dil // W

        def blk_body(blk, _, p=p, dil=dil, nb=nb):
            r = blk // nb
            n = blk % nb
            start = r + n * (W * dil)
            rows = pl.ds(start, W, stride=dil)
            qb = qf[rows, :].astype(bf16)
            ko = kf[rows, :].astype(bf16)
            vo = vf[rows, :].astype(bf16)
            if nb > 1:
                prow = pl.ds(jnp.maximum(start - W * dil, r), W, stride=dil)
                kp = kf[prow, :].astype(bf16)
                vp = vf[prow, :].astype(bf16)
            outs, lses = [], []
            for hh in range(2):
                sl = slice(hh * HEAD_DIM, (hh + 1) * HEAD_DIM)
                s_o = jnp.where(own_ok, _dot_nt(qb[:, sl], ko[:, sl]) + bown[p, hh], NEG_INF)
                m = jnp.max(s_o, axis=1, keepdims=True)
                if nb > 1:
                    s_p = jnp.where(prev_ok, _dot_nt(qb[:, sl], kp[:, sl]) + bprev[p, hh], NEG_INF)
                    s_p = jnp.where(n > 0, s_p, NEG_INF)
                    m = jnp.maximum(m, jnp.max(s_p, axis=1, keepdims=True))
                e_o = jnp.exp(s_o - m)
                den = jnp.sum(e_o, axis=1, keepdims=True)
                num = _dot(e_o.astype(bf16), vo[:, sl])
                if nb > 1:
                    e_p = jnp.exp(s_p - m)
                    den = den + jnp.sum(e_p, axis=1, keepdims=True)
                    num = num + _dot(e_p.astype(bf16), vp[:, sl])
                outs.append(num / den)
                lses.append(jnp.broadcast_to(m + jnp.log(den), (W, HEAD_DIM)))
            osc[p, rows, :] = jnp.concatenate(outs, axis=1)
            lsc[p, rows, :] = jnp.concatenate(lses, axis=1)
            return 0

        lax.fori_loop(0, SEQ // W, blk_body, 0)

    def merge_body(i, _):
        rows = pl.ds(pl.multiple_of(i * ATT_TILE, ATT_TILE), ATT_TILE)
        ls = [lsc[p, rows, :] for p in range(len(DILATED_PATTERNS))]
        mx = functools.reduce(jnp.maximum, ls)
        ws = [jnp.exp(l - mx) for l in ls]
        tot = functools.reduce(lambda a, b: a + b, ws)
        out = functools.reduce(lambda a, b: a + b,
                               [(w / tot) * osc[p, rows, :] for p, w in enumerate(ws)])
        o_ref[0, rows, :] = out.astype(bf16)
        return 0

    lax.fori_loop(0, N_ATT_TILES, merge_body, 0)


def _dilated_attention(qkv, table):
    B = qkv.shape[0]
    nblk = HALF // LANES
    first = 3 * nblk
    npat = len(DILATED_PATTERNS)
    return pl.pallas_call(
        _dil_kernel,
        grid=(N_HEADS_B // 2, B),
        in_specs=[_smem_spec(), _head_spec(first), _head_spec(first + nblk),
                  _head_spec(first + 2 * nblk)],
        out_specs=_head_spec(0),
        out_shape=jax.ShapeDtypeStruct((B, SEQ, HALF), bf16),
        scratch_shapes=[
            pltpu.VMEM((SEQ, LANES), f32), pltpu.VMEM((SEQ, LANES), f32), pltpu.VMEM((SEQ, LANES), f32),
            pltpu.VMEM((npat, 2, DIL_BLOCK, DIL_BLOCK), f32),
            pltpu.VMEM((npat, 2, DIL_BLOCK, DIL_BLOCK), f32),
            pltpu.VMEM((npat, SEQ, LANES), f32), pltpu.VMEM((npat, SEQ, LANES), f32),
        ],
        compiler_params=pltpu.CompilerParams(
            dimension_semantics=("arbitrary", "arbitrary"), vmem_limit_bytes=VMEM_LIMIT),
        name="dilated_attention",
    )(table, qkv, qkv, qkv)


def _moba_kernel(tbl_ref, q_ref, k_ref, v_ref, o_ref, bias_sc, vt_sc):
    hp = pl.program_id(0)
    assert MOBA_BLOCK == ATT_TILE == KEY_CHUNK

    @pl.when(pl.program_id(1) == 0)
    def _():
        for hh in range(2):
            bias_sc[hh] = _near_bias_t(tbl_ref, 2 * hp + hh)

    _store_transposed(v_ref, vt_sc)

    def load_q(hh, queries):
        return q_ref[0, queries, hh * HEAD_DIM:(hh + 1) * HEAD_DIM]

    def load_k(hh, keys):
        return k_ref[0, keys, hh * HEAD_DIM:(hh + 1) * HEAD_DIM]

    def load_vt(hh, keys):
        return vt_sc[hh * HEAD_DIM:(hh + 1) * HEAD_DIM, keys]

    k_means = []
    for hh in range(2):
        means = [jnp.sum(load_k(hh, slice(n * MOBA_BLOCK, (n + 1) * MOBA_BLOCK)).astype(f32), axis=0,
                         keepdims=True) * (1.0 / MOBA_BLOCK) for n in range(N_MOBA_BLOCKS)]
        k_means.append(jnp.concatenate(means + [jnp.zeros_like(m) for m in means], axis=0).astype(bf16))

    block_masks = {}

    def block_mask(hh, queries, blk):
        own = queries.start // MOBA_BLOCK
        if own <= MOBA_TOPK:
            return None
        if (hh, own) not in block_masks:
            gate = _dot_nt(k_means[hh], load_q(hh, queries))
            g = [gate[n:n + 1, :] for n in range(own)]
            masks = []
            for n in range(own):
                rank = jnp.zeros((1, ATT_TILE), jnp.int32)
                for m in range(own):
                    if m != n:
                        beats = (g[m] >= g[n]) if m < n else (g[m] > g[n])
                        rank = rank + beats.astype(jnp.int32)
                masks.append(jnp.where(rank < MOBA_TOPK, 0.0, NEG_INF))
            block_masks[(hh, own)] = masks
        return block_masks[(hh, own)][blk]

    def score_mod(hh, queries, keys, t):
        t = _add_near_bias(t, bias_sc, (hh,), queries, keys)
        if keys.start < queries.start:
            mask = block_mask(hh, queries, keys.start // MOBA_BLOCK)
            if mask is not None:
                t = t + mask
        return t

    def emit(queries, results):
        o_t = jnp.concatenate([acc / l for acc, l in results], axis=0)
        o_ref[0, queries, :] = o_t.T.astype(bf16)

    _causal_flash_t(range(2), load_q, load_k, load_vt, score_mod, lambda hh, queries: None, emit)


def _moba_attention(qkv, table):
    B = qkv.shape[0]
    nblk = HALF // LANES
    return pl.pallas_call(
        _moba_kernel,
        grid=(N_HEADS_C // 2, B),
        in_specs=[_smem_spec(), _head_spec(0), _head_spec(nblk), _head_spec(2 * nblk)],
        out_specs=_head_spec(0),
        out_shape=jax.ShapeDtypeStruct((B, SEQ, HALF), bf16),
        scratch_shapes=[pltpu.VMEM((2, 2 * ATT_TILE, ATT_TILE), f32),
                        pltpu.VMEM((LANES, SEQ), bf16)],
        compiler_params=pltpu.CompilerParams(
            dimension_semantics=("arbitrary", "arbitrary"), vmem_limit_bytes=VMEM_LIMIT),
        name="moba_attention",
    )(table, qkv, qkv, qkv)


def _forget_cum_kernel(fd_ref, fb_ref, col_ref, row_ref):
    x = jax.nn.log_sigmoid(fd_ref[0] + fb_ref[...])
    t = lax.broadcasted_iota(jnp.int32, (SEQ, LANES), 0)
    shift = 1
    while shift < SEQ:
        x = x + jnp.where(t >= shift, pltpu.roll(x, shift, axis=0), 0.0)
        shift *= 2
    col_ref[0] = x
    for c in range(SEQ // LANES):
        blk = x[c * LANES:(c + 1) * LANES, :].T
        row_ref[0, :, c * LANES:(c + 1) * LANES] = blk[0:N_HEADS_D, :]


def _forget_cum(fd, forget_b):
    B = fd.shape[0]
    fb = jnp.zeros((1, LANES), f32).at[0, :N_HEADS_D].set(forget_b)
    return pl.pallas_call(
        _forget_cum_kernel,
        grid=(B,),
        in_specs=[pl.BlockSpec((1, SEQ, LANES), lambda b: (b, 0, 0)),
                  pl.BlockSpec((1, LANES), lambda b: (0, 0))],
        out_specs=[pl.BlockSpec((1, SEQ, LANES), lambda b: (b, 0, 0)),
                   pl.BlockSpec((1, N_HEADS_D, SEQ), lambda b: (b, 0, 0))],
        out_shape=[jax.ShapeDtypeStruct((B, SEQ, LANES), f32),
                   jax.ShapeDtypeStruct((B, N_HEADS_D, SEQ), f32)],
        compiler_params=pltpu.CompilerParams(
            dimension_semantics=("arbitrary",), vmem_limit_bytes=VMEM_LIMIT),
        name="forget_cum",
    )(fd, fb)


def _fox_kernel(ccol_ref, crow_ref, q_ref, k_ref, v_ref, o_ref, col_sc, vt_sc):
    hp = pl.program_id(0)
    lane = lax.broadcasted_iota(jnp.int32, (SEQ, LANES), 1)
    for hh in range(2):
        col_sc[hh] = jnp.sum(jnp.where(lane == 2 * hp + hh, ccol_ref[0], 0.0), axis=1,
                             keepdims=True)
    _store_transposed(v_ref, vt_sc)

    def load_q(hh, queries):
        return q_ref[0, queries, hh * HEAD_DIM:(hh + 1) * HEAD_DIM]

    def load_k(hh, keys):
        return k_ref[0, keys, hh * HEAD_DIM:(hh + 1) * HEAD_DIM]

    def load_vt(hh, keys):
        return vt_sc[hh * HEAD_DIM:(hh + 1) * HEAD_DIM, keys]

    def score_mod(hh, queries, keys, t):
        return t - col_sc[hh, keys, :]

    def query_shift(hh, queries):
        return crow_ref[0, pl.ds(2 * hp + hh, 1), queries]

    def emit(queries, results):
        o_t = jnp.concatenate([acc / l for acc, l in results], axis=0)
        o_ref[0, queries, :] = o_t.T.astype(bf16)

    _causal_flash_t(range(2), load_q, load_k, load_vt, score_mod, query_shift, emit)


def _fox_attention(qkv, cum_col, cum_row):
    B = qkv.shape[0]
    nblk = HALF // LANES
    first = 3 * nblk
    return pl.pallas_call(
        _fox_kernel,
        grid=(N_HEADS_D // 2, B),
        in_specs=[pl.BlockSpec((1, SEQ, LANES), lambda g, b: (b, 0, 0)),
                  pl.BlockSpec((1, N_HEADS_D, SEQ), lambda g, b: (b, 0, 0)),
                  _head_spec(first), _head_spec(first + nblk), _head_spec(first + 2 * nblk)],
        out_specs=_head_spec(0),
        out_shape=jax.ShapeDtypeStruct((B, SEQ, HALF), bf16),
        scratch_shapes=[pltpu.VMEM((2, SEQ, 1), f32), pltpu.VMEM((LANES, SEQ), bf16)],
        compiler_params=pltpu.CompilerParams(
            dimension_semantics=("arbitrary", "arbitrary"), vmem_limit_bytes=VMEM_LIMIT),
        name="fox_attention",
    )(cum_col, cum_row, qkv, qkv, qkv)


def _post_kernel(o1_ref, o2_ref, x_ref, g1_ref, sc2_ref, sh2_ref, g2_ref, ln_ref,
                 wo_ref, wi_ref, wout_ref, out_ref):
    y = _dot(o1_ref[0], wo_ref[0:HALF, :]) + _dot(o2_ref[0], wo_ref[HALF:, :])
    ln = ln_ref[...]
    x1 = _layer_norm(DEEPNORM_ALPHA * x_ref[0] + (1.0 + g1_ref[0]) * y, ln[0:1], ln[1:2])
    h = (x1 * (1.0 + sc2_ref[0]) + sh2_ref[0]).astype(bf16)
    y2 = jnp.zeros((ROW_TILE, D_MODEL), f32)
    for c in range(D_FF // FF_CHUNK):
        g = _dot(h, wi_ref[:, c * FF_CHUNK:(c + 1) * FF_CHUNK])
        u = _dot(h, wi_ref[:, D_FF + c * FF_CHUNK:D_FF + (c + 1) * FF_CHUNK])
        a = (g * jax.nn.sigmoid(g) * u).astype(bf16)
        y2 = y2 + _dot(a, wout_ref[c * FF_CHUNK:(c + 1) * FF_CHUNK, :])
    out_ref[0] = _layer_norm(DEEPNORM_ALPHA * x1 + (1.0 + g2_ref[0]) * y2, ln[2:3], ln[3:4])


def _post(o1, o2, x, g1, sc2, sh2, g2, ln, w_o, w_ffn_in, w_ffn_out):
    B = x.shape[0]
    return pl.pallas_call(
        _post_kernel,
        grid=(B, SEQ // ROW_TILE),
        in_specs=[_row_spec(HALF), _row_spec(HALF), _row_spec(D_MODEL),
                  _batch_vec_spec(), _batch_vec_spec(), _batch_vec_spec(), _batch_vec_spec(),
                  pl.BlockSpec((4, D_MODEL), lambda b, i: (0, 0)),
                  _resident(w_o.shape), _resident(w_ffn_in.shape), _resident(w_ffn_out.shape)],
        out_specs=_row_spec(D_MODEL),
        out_shape=jax.ShapeDtypeStruct((B, SEQ, D_MODEL), f32),
        compiler_params=pltpu.CompilerParams(
            dimension_semantics=("arbitrary", "arbitrary"), vmem_limit_bytes=VMEM_LIMIT),
        name="out_proj_ffn",
    )(o1, o2, x, g1, sc2, sh2, g2, ln, w_o, w_ffn_in, w_ffn_out)


def _scaled_in_weights(w):
    scale = np.ones((IN_MAIN,), np.float32)
    scale[0:HALF] = ATTN_SCALE
    scale[3 * HALF:4 * HALF] = ATTN_SCALE
    return (w[:, :IN_MAIN] * scale).astype(bf16)


def kernel(x, c, rel_bias, w_ada, b_ada, ln_g, ln_b, w_in_ab, diff_lambda, diff_subln_g,
           w_in_cd, forget_b, w_o, w_ffn_in, w_ffn_out):
    B = x.shape[0]
    ada = _ada(c, w_ada, b_ada)
    for l in range(DEPTH):
        sh1, sc1, g1, sh2, sc2, g2 = [a.reshape(B, 1, D_MODEL) for a in jnp.split(ada[l], 6, axis=-1)]
        i = l // 2
        if l % 2 == 0:
            lambda_init = 0.8 - 0.6 * math.exp(-0.3 * l)
            qkv = _in_proj(x, sc1, sh1, _scaled_in_weights(w_in_ab[i]))
            o1 = _diff_attention(qkv, rel_bias, diff_lambda[i], diff_subln_g[i], lambda_init)
            o2 = _dilated_attention(qkv, rel_bias)
        else:
            w_gate = jnp.zeros((D_MODEL, LANES), f32).at[:, :N_HEADS_D].set(w_in_cd[i][:, IN_MAIN:])
            qkv, fd = _in_proj(x, sc1, sh1, _scaled_in_weights(w_in_cd[i]), w_gate.astype(bf16))
            cum_col, cum_row = _forget_cum(fd, forget_b[i])
            o1 = _moba_attention(qkv, rel_bias)
            o2 = _fox_attention(qkv, cum_col, cum_row)
        ln = jnp.stack([ln_g[l, 0], ln_b[l, 0], ln_g[l, 1], ln_b[l, 1]])
        x = _post(o1, o2, x, g1, sc2, sh2, g2, ln, w_o[l].astype(bf16),
                  w_ffn_in[l].astype(bf16), w_ffn_out[l].astype(bf16))
    return x
```

```python
import functools
import math

import numpy as np
import jax
import jax.numpy as jnp
from jax import lax
from jax.experimental import pallas as pl
from jax.experimental.pallas import tpu as pltpu

f32 = jnp.float32
bf16 = jnp.bfloat16

D_MODEL = 1024
SEQ = 2048
DEPTH = 2
HEAD_DIM = 64
N_HEADS_A = 4
DV_A = 2 * HEAD_DIM
N_HEADS_B = 8
DILATED_PATTERNS = ((128, 1), (512, 4), (2048, 16))
DIL_BLOCK = 128
N_HEADS_C = 8
MOBA_BLOCK = 256
MOBA_TOPK = 3
N_MOBA_BLOCKS = SEQ // MOBA_BLOCK
N_HEADS_D = 8
NUM_BUCKETS = 32
MAX_DISTANCE = 128
D_FF = 2816
LN_EPS = 1e-5
DEEPNORM_ALPHA = (2 * DEPTH) ** 0.25
ATTN_SCALE = HEAD_DIM ** -0.5
HALF = D_MODEL // 2
IN_MAIN = 3 * D_MODEL

LANES = 128
ATT_TILE = 256
KEY_CHUNK = 256
N_ATT_TILES = SEQ // ATT_TILE
PIPELINE_DEPTH = 4
FF_CHUNK = D_FF // 2
ROW_TILE = 512
VMEM_LIMIT = 56 * 1024 * 1024

NEG_INF = float("-inf")


def _t5_bucket_starts():
    n = np.arange(0, 4 * SEQ, dtype=np.int64)
    max_exact = NUM_BUCKETS // 2
    nf = np.maximum(n, 1).astype(np.float32)
    val = (np.log(nf / np.float32(max_exact)) / np.float32(math.log(MAX_DISTANCE / max_exact))
           * np.float32(NUM_BUCKETS - max_exact))
    large = np.minimum(max_exact + val.astype(np.int32), NUM_BUCKETS - 1)
    bucket = np.where(n < max_exact, n, large)
    assert np.all(np.diff(bucket) >= 0)
    return tuple(int(np.argmax(bucket >= b)) for b in range(NUM_BUCKETS))


T5_STARTS = _t5_bucket_starts()
assert T5_STARTS[-1] <= ATT_TILE + 1


def _t5_bias(dist, tbl_ref, col):
    out = jnp.full(dist.shape, tbl_ref[0, col], f32)
    for b in range(1, NUM_BUCKETS):
        out = jnp.where(dist >= T5_STARTS[b], tbl_ref[b, col], out)
    return out


def _dot_nt(a, b):
    return lax.dot_general(a, b, (((1,), (1,)), ((), ())), preferred_element_type=f32)


def _dot(a, b):
    return jnp.dot(a, b, preferred_element_type=f32)


def _software_pipeline(items, first_stage, second_stage):
    pending = [first_stage(it) for it in items[:PIPELINE_DEPTH]]
    for idx, it in enumerate(items):
        cur = pending.pop(0)
        if idx + PIPELINE_DEPTH < len(items):
            pending.append(first_stage(items[idx + PIPELINE_DEPTH]))
        second_stage(it, cur)


def _store_transposed(v_ref, vt_ref):
    for c in range(N_ATT_TILES):
        rows = slice(c * ATT_TILE, (c + 1) * ATT_TILE)
        vt_ref[:, rows] = v_ref[0, rows, :].astype(f32).T.astype(bf16)


def _causal_flash_t(streams, load_q, load_k, load_vt, score_mod, query_shift, emit):
    streams = list(streams)
    krow = lax.broadcasted_iota(jnp.int32, (KEY_CHUNK, ATT_TILE), 0)
    qcol = lax.broadcasted_iota(jnp.int32, (KEY_CHUNK, ATT_TILE), 1)
    per_tile = ATT_TILE // KEY_CHUNK
    state = {}

    def spans(work):
        _, i, c = work
        return (slice(i * ATT_TILE, (i + 1) * ATT_TILE), slice(c * KEY_CHUNK, (c + 1) * KEY_CHUNK))

    def first_stage(work):
        s, i, c = work
        queries, keys = spans(work)
        t = score_mod(s, queries, keys, _dot_nt(load_k(s, keys), load_q(s, queries)))
        if keys.stop - 1 > queries.start:
            t = jnp.where(krow + (keys.start - queries.start) <= qcol, t, NEG_INF)
        return t, jnp.max(t, axis=0, keepdims=True)

    def second_stage(work, staged):
        s, i, c = work
        queries, keys = spans(work)
        t, tmax = staged
        m, l, acc = state.get((s, i), (None, None, None))
        m_new = tmax if m is None else jnp.maximum(m, tmax)
        shift = query_shift(s, queries)
        if shift is None:
            p = jnp.exp(t - m_new)
        else:
            p = jnp.exp(t + (shift - (m_new + shift)))
        psum = jnp.sum(p, axis=0, keepdims=True)
        pv = _dot(load_vt(s, keys), p.astype(bf16))
        if m is None:
            l, acc = psum, pv
        else:
            alpha = jnp.exp(m - m_new)
            l, acc = alpha * l + psum, alpha * acc + pv
        state[(s, i)] = (m_new, l, acc)
        if (s, c) == last_of_tile[i]:
            emit(queries, [(state[(s2, i)][2], state[(s2, i)][1]) for s2 in streams])

    work, last_of_tile = [], {}
    for i in reversed(range(N_ATT_TILES)):
        first = i * per_tile
        order = [first] + [c for c in range((i + 1) * per_tile) if c != first]
        work += [(s, i, c) for c in order for s in streams]
        last_of_tile[i] = (streams[-1], order[-1])
    _software_pipeline(work, first_stage, second_stage)


def _tile_iotas(n):
    row = lax.broadcasted_iota(jnp.int32, (n, n), 0)
    col = lax.broadcasted_iota(jnp.int32, (n, n), 1)
    return row, col


def _layer_norm(x, g, b):
    mu = jnp.mean(x, axis=1, keepdims=True)
    xc = x - mu
    var = jnp.mean(xc * xc, axis=1, keepdims=True)
    return xc * lax.rsqrt(var + LN_EPS) * g + b


def _ada_kernel(c_ref, w_ref, b_ref, o_ref):
    c = c_ref[...]
    cs = c * jax.nn.sigmoid(c)
    o_ref[0] = _dot(cs.astype(bf16), w_ref[0].astype(bf16)) + b_ref[0]


def _ada(c, w_ada, b_ada):
    B = c.shape[0]
    tn = 1536
    return pl.pallas_call(
        _ada_kernel,
        grid=(DEPTH, 6 * D_MODEL // tn),
        in_specs=[
            pl.BlockSpec((B, D_MODEL), lambda l, j: (0, 0)),
            pl.BlockSpec((1, D_MODEL, tn), lambda l, j: (l, 0, j)),
            pl.BlockSpec((1, 1, tn), lambda l, j: (l, 0, j)),
        ],
        out_specs=pl.BlockSpec((1, B, tn), lambda l, j: (l, 0, j)),
        out_shape=jax.ShapeDtypeStruct((DEPTH, B, 6 * D_MODEL), f32),
        compiler_params=pltpu.CompilerParams(
            dimension_semantics=("arbitrary", "arbitrary"), vmem_limit_bytes=VMEM_LIMIT),
        name="ada",
    )(c, w_ada, b_ada.reshape(DEPTH, 1, 6 * D_MODEL))


def _proj_kernel(x_ref, sc_ref, sh_ref, w_ref, o_ref):
    h = (x_ref[0] * (1.0 + sc_ref[0]) + sh_ref[0]).astype(bf16)
    o_ref[0] = _dot(h, w_ref[...]).astype(bf16)


def _proj_gate_kernel(x_ref, sc_ref, sh_ref, w_ref, wf_ref, o_ref, f_ref):
    h = (x_ref[0] * (1.0 + sc_ref[0]) + sh_ref[0]).astype(bf16)
    o_ref[0] = _dot(h, w_ref[...]).astype(bf16)
    f_ref[0] = _dot(h, wf_ref[...])


def _resident(shape):
    zeros = (0,) * len(shape)
    return pl.BlockSpec(shape, lambda *_: zeros, pipeline_mode=pl.Buffered(1))


def _row_spec(width):
    return pl.BlockSpec((1, ROW_TILE, width), lambda b, i: (b, i, 0))


def _batch_vec_spec():
    return pl.BlockSpec((1, 1, D_MODEL), lambda b, i: (b, 0, 0))


def _in_proj(x, sc, sh, w, w_gate=None):
    B = x.shape[0]
    in_specs = [_row_spec(D_MODEL), _batch_vec_spec(), _batch_vec_spec(), _resident(w.shape)]
    out_specs = [_row_spec(IN_MAIN)]
    out_shape = [jax.ShapeDtypeStruct((B, SEQ, IN_MAIN), bf16)]
    args = [x, sc, sh, w]
    kern = _proj_kernel
    if w_gate is not None:
        in_specs.append(_resident(w_gate.shape))
        out_specs.append(_row_spec(LANES))
        out_shape.append(jax.ShapeDtypeStruct((B, SEQ, LANES), f32))
        args.append(w_gate)
        kern = _proj_gate_kernel
    res = pl.pallas_call(
        kern,
        grid=(B, SEQ // ROW_TILE),
        in_specs=in_specs,
        out_specs=out_specs,
        out_shape=out_shape,
        compiler_params=pltpu.CompilerParams(
            dimension_semantics=("arbitrary", "arbitrary"), vmem_limit_bytes=VMEM_LIMIT),
        name="in_proj",
    )(*args)
    return res if w_gate is not None else res[0]


def _near_bias_t(tbl_ref, head):
    kk = lax.broadcasted_iota(jnp.int32, (2 * ATT_TILE, ATT_TILE), 0)
    qq = lax.broadcasted_iota(jnp.int32, (2 * ATT_TILE, ATT_TILE), 1)
    return _t5_bias(qq - kk + ATT_TILE, tbl_ref, head) - tbl_ref[NUM_BUCKETS - 1, head]


def _add_near_bias(t, bias_ref, lead, queries, keys):
    rel = keys.start - (queries.start - ATT_TILE)
    if rel < 0:
        return t
    return t + bias_ref[lead + (slice(rel, rel + KEY_CHUNK), slice(None))]


def _smem_spec():
    return pl.BlockSpec(memory_space=pltpu.SMEM)


def _head_spec(first_block):
    return pl.BlockSpec((1, SEQ, LANES), lambda g, b: (b, 0, first_block + g))


def _diff_kernel(tbl_ref, lam_ref, g_ref, q_ref, k_ref, v_ref, o_ref, bias_sc, vt_sc, *,
                 lambda_init):
    h = pl.program_id(0)

    @pl.when(pl.program_id(1) == 0)
    def _():
        bias_sc[...] = _near_bias_t(tbl_ref, h)

    _store_transposed(v_ref, vt_sc)
    lm = lam_ref[...]
    lam_full = (jnp.exp(jnp.sum(lm[0:1] * lm[1:2], axis=1, keepdims=True))
                - jnp.exp(jnp.sum(lm[2:3] * lm[3:4], axis=1, keepdims=True)) + lambda_init)

    def load_q(m, queries):
        return q_ref[0, queries, m * HEAD_DIM:(m + 1) * HEAD_DIM]

    def load_k(m, keys):
        return k_ref[0, keys, m * HEAD_DIM:(m + 1) * HEAD_DIM]

    def load_vt(m, keys):
        return vt_sc[:, keys]

    def score_mod(m, queries, keys, t):
        return _add_near_bias(t, bias_sc, (), queries, keys)

    def emit(queries, results):
        (acc1, l1), (acc2, l2) = results
        o = (acc1 / l1 - lam_full * (acc2 / l2)).T
        o = o * lax.rsqrt(jnp.mean(o * o, axis=1, keepdims=True) + LN_EPS)
        o_ref[0, queries, :] = (o * g_ref[...] * (1.0 - lambda_init)).astype(bf16)

    _causal_flash_t(range(2), load_q, load_k, load_vt, score_mod, lambda m, queries: None, emit)


def _diff_attention(qkv, table, lam, subln_g, lambda_init):
    B = qkv.shape[0]
    nblk = HALF // LANES
    return pl.pallas_call(
        functools.partial(_diff_kernel, lambda_init=lambda_init),
        grid=(N_HEADS_A, B),
        in_specs=[
            _smem_spec(),
            pl.BlockSpec((4, HEAD_DIM), lambda g, b: (0, 0)),
            pl.BlockSpec((1, DV_A), lambda g, b: (0, 0)),
            _head_spec(0), _head_spec(nblk), _head_spec(2 * nblk),
        ],
        out_specs=_head_spec(0),
        out_shape=jax.ShapeDtypeStruct((B, SEQ, HALF), bf16),
        scratch_shapes=[pltpu.VMEM((2 * ATT_TILE, ATT_TILE), f32), pltpu.VMEM((LANES, SEQ), bf16)],
        compiler_params=pltpu.CompilerParams(
            dimension_semantics=("arbitrary", "arbitrary"), vmem_limit_bytes=VMEM_LIMIT),
        name="diff_attention",
    )(table, lam, subln_g.reshape(1, DV_A), qkv, qkv, qkv)


def _dil_kernel(tbl_ref, q_ref, k_ref, v_ref, o_ref, qf, kf, vf, bias_sc, osc, lsc):
    hp = pl.program_id(0)
    W = DIL_BLOCK
    npat = len(DILATED_PATTERNS)
    row = lax.broadcasted_iota(jnp.int32, (W, 2 * W), 0)
    col = lax.broadcasted_iota(jnp.int32, (W, 2 * W), 1)
    du = row - col + W

    @pl.when(pl.program_id(1) == 0)
    def _():
        for p, (_, dil) in enumerate(DILATED_PATTERNS):
            for hh in range(2):
                bias = _t5_bias(du * dil, tbl_ref, N_HEADS_A + 2 * hp + hh)
                bias_sc[p, hh] = jnp.where((du >= 0) & (du <= W), bias, NEG_INF)

    qf[...] = q_ref[0].astype(f32)
    kf[...] = k_ref[0].astype(f32)
    vf[...] = v_ref[0].astype(f32)
    lane = lax.broadcasted_iota(jnp.int32, (W, LANES), 1)
    head_lanes = [lane < HEAD_DIM, lane >= HEAD_DIM]
    blocks, staged_out = {}, {}

    def block(p, r, n):
        if (p, r, n) not in blocks:
            dil = DILATED_PATTERNS[p][1]
            rows = pl.ds(r + n * W * dil, W, stride=dil)
            blocks[(p, r, n)] = tuple(ref[rows, :].astype(bf16) for ref in (qf, kf, vf))
        return blocks[(p, r, n)]

    def scores(item):
        p, r, n, hh = item
        q, k, _ = block(p, r, n)
        q = jnp.where(head_lanes[hh], q, jnp.zeros_like(q))
        if n > 0:
            k = jnp.concatenate([block(p, r, n - 1)[1], k], axis=0)
            s = _dot_nt(q, k) + bias_sc[p, hh]
        else:
            s = _dot_nt(q, k) + bias_sc[p, hh, :, W:]
        return s, jnp.max(s, axis=1, keepdims=True)

    def finish(item, staged):
        p, r, n, hh = item
        s, m = staged
        v = block(p, r, n)[2]
        if n > 0:
            v = jnp.concatenate([block(p, r, n - 1)[2], v], axis=0)
        e = jnp.exp(s - m)
        den = jnp.sum(e, axis=1, keepdims=True)
        num = _dot(e.astype(bf16), v)
        staged_out[hh] = (num / den, m + jnp.log(den))
        if hh == 1:
            dil = DILATED_PATTERNS[p][1]
            rows = pl.ds(r + n * W * dil, W, stride=dil)
            (o0, l0), (o1, l1) = staged_out[0], staged_out[1]
            osc[p, rows, :] = jnp.where(head_lanes[0], o0, o1)
            lsc[p, rows, :] = jnp.where(head_lanes[0], l0, l1)

    items = []
    for p, (window, dil) in enumerate(DILATED_PATTERNS):
        assert window // dil == W
        nb = SEQ // dil // W
        items += [(p, r, n, hh) for r in range(dil) for n in range(nb) for hh in range(2)]
    _software_pipeline(items, scores, finish)

    def merge_body(i, _):
        rows = pl.ds(pl.multiple_of(i * ATT_TILE, ATT_TILE), ATT_TILE)
        ls = [lsc[p, rows, :] for p in range(npat)]
        mx = functools.reduce(jnp.maximum, ls)
        ws = [jnp.exp(l - mx) for l in ls]
        tot = functools.reduce(lambda a, b: a + b, ws)
        out = functools.reduce(lambda a, b: a + b,
                               [(w / tot) * osc[p, rows, :] for p, w in enumerate(ws)])
        o_ref[0, rows, :] = out.astype(bf16)
        return 0

    lax.fori_loop(0, N_ATT_TILES, merge_body, 0)


def _dilated_attention(qkv, table):
    B = qkv.shape[0]
    nblk = HALF // LANES
    first = 3 * nblk
    npat = len(DILATED_PATTERNS)
    return pl.pallas_call(
        _dil_kernel,
        grid=(N_HEADS_B // 2, B),
        in_specs=[_smem_spec(), _head_spec(first), _head_spec(first + nblk),
                  _head_spec(first + 2 * nblk)],
        out_specs=_head_spec(0),
        out_shape=jax.ShapeDtypeStruct((B, SEQ, HALF), bf16),
        scratch_shapes=[
            pltpu.VMEM((SEQ, LANES), f32), pltpu.VMEM((SEQ, LANES), f32), pltpu.VMEM((SEQ, LANES), f32),
            pltpu.VMEM((npat, 2, DIL_BLOCK, 2 * DIL_BLOCK), f32),
            pltpu.VMEM((npat, SEQ, LANES), f32), pltpu.VMEM((npat, SEQ, LANES), f32),
        ],
        compiler_params=pltpu.CompilerParams(
            dimension_semantics=("arbitrary", "arbitrary"), vmem_limit_bytes=VMEM_LIMIT),
        name="dilated_attention",
    )(table, qkv, qkv, qkv)


def _moba_kernel(tbl_ref, q_ref, k_ref, v_ref, o_ref, bias_sc, vt_sc):
    hp = pl.program_id(0)
    assert MOBA_BLOCK == ATT_TILE == KEY_CHUNK

    @pl.when(pl.program_id(1) == 0)
    def _():
        for hh in range(2):
            bias_sc[hh] = _near_bias_t(tbl_ref, 2 * hp + hh)

    _store_transposed(v_ref, vt_sc)

    def load_q(hh, queries):
        return q_ref[0, queries, hh * HEAD_DIM:(hh + 1) * HEAD_DIM]

    def load_k(hh, keys):
        return k_ref[0, keys, hh * HEAD_DIM:(hh + 1) * HEAD_DIM]

    def load_vt(hh, keys):
        return vt_sc[hh * HEAD_DIM:(hh + 1) * HEAD_DIM, keys]

    k_means = []
    for hh in range(2):
        means = [jnp.sum(load_k(hh, slice(n * MOBA_BLOCK, (n + 1) * MOBA_BLOCK)).astype(f32), axis=0,
                         keepdims=True) * (1.0 / MOBA_BLOCK) for n in range(N_MOBA_BLOCKS)]
        k_means.append(jnp.concatenate(means + [jnp.zeros_like(m) for m in means], axis=0).astype(bf16))

    block_masks = {}

    def block_mask(hh, queries, blk):
        own = queries.start // MOBA_BLOCK
        if own <= MOBA_TOPK:
            return None
        if (hh, own) not in block_masks:
            gate = _dot_nt(k_means[hh], load_q(hh, queries))
            g = [gate[n:n + 1, :] for n in range(own)]
            masks = []
            for n in range(own):
                rank = jnp.zeros((1, ATT_TILE), jnp.int32)
                for m in range(own):
                    if m != n:
                        beats = (g[m] >= g[n]) if m < n else (g[m] > g[n])
                        rank = rank + beats.astype(jnp.int32)
                masks.append(jnp.where(rank < MOBA_TOPK, 0.0, NEG_INF))
            block_masks[(hh, own)] = masks
        return block_masks[(hh, own)][blk]

    def score_mod(hh, queries, keys, t):
        t = _add_near_bias(t, bias_sc, (hh,), queries, keys)
        if keys.start < queries.start:
            mask = block_mask(hh, queries, keys.start // MOBA_BLOCK)
            if mask is not None:
                t = t + mask
        return t

    def emit(queries, results):
        o_t = jnp.concatenate([acc / l for acc, l in results], axis=0)
        o_ref[0, queries, :] = o_t.T.astype(bf16)

    _causal_flash_t(range(2), load_q, load_k, load_vt, score_mod, lambda hh, queries: None, emit)


def _moba_attention(qkv, table):
    B = qkv.shape[0]
    nblk = HALF // LANES
    return pl.pallas_call(
        _moba_kernel,
        grid=(N_HEADS_C // 2, B),
        in_specs=[_smem_spec(), _head_spec(0), _head_spec(nblk), _head_spec(2 * nblk)],
        out_specs=_head_spec(0),
        out_shape=jax.ShapeDtypeStruct((B, SEQ, HALF), bf16),
        scratch_shapes=[pltpu.VMEM((2, 2 * ATT_TILE, ATT_TILE), f32),
                        pltpu.VMEM((LANES, SEQ), bf16)],
        compiler_params=pltpu.CompilerParams(
            dimension_semantics=("arbitrary", "arbitrary"), vmem_limit_bytes=VMEM_LIMIT),
        name="moba_attention",
    )(table, qkv, qkv, qkv)


def _forget_cum_kernel(fd_ref, fb_ref, col_ref, row_ref):
    x = jax.nn.log_sigmoid(fd_ref[0] + fb_ref[...])
    t = lax.broadcasted_iota(jnp.int32, (SEQ, LANES), 0)
    shift = 1
    while shift < SEQ:
        x = x + jnp.where(t >= shift, pltpu.roll(x, shift, axis=0), 0.0)
        shift *= 2
    col_ref[0] = x
    for c in range(SEQ // LANES):
        blk = x[c * LANES:(c + 1) * LANES, :].T
        row_ref[0, :, c * LANES:(c + 1) * LANES] = blk[0:N_HEADS_D, :]


def _forget_cum(fd, forget_b):
    B = fd.shape[0]
    fb = jnp.zeros((1, LANES), f32).at[0, :N_HEADS_D].set(forget_b)
    return pl.pallas_call(
        _forget_cum_kernel,
        grid=(B,),
        in_specs=[pl.BlockSpec((1, SEQ, LANES), lambda b: (b, 0, 0)),
                  pl.BlockSpec((1, LANES), lambda b: (0, 0))],
        out_specs=[pl.BlockSpec((1, SEQ, LANES), lambda b: (b, 0, 0)),
                   pl.BlockSpec((1, N_HEADS_D, SEQ), lambda b: (b, 0, 0))],
        out_shape=[jax.ShapeDtypeStruct((B, SEQ, LANES), f32),
                   jax.ShapeDtypeStruct((B, N_HEADS_D, SEQ), f32)],
        compiler_params=pltpu.CompilerParams(
            dimension_semantics=("arbitrary",), vmem_limit_bytes=VMEM_LIMIT),
        name="forget_cum",
    )(fd, fb)


def _fox_kernel(ccol_ref, crow_ref, q_ref, k_ref, v_ref, o_ref, col_sc, vt_sc):
    hp = pl.program_id(0)
    lane = lax.broadcasted_iota(jnp.int32, (SEQ, LANES), 1)
    for hh in range(2):
        col_sc[hh] = jnp.sum(jnp.where(lane == 2 * hp + hh, ccol_ref[0], 0.0), axis=1,
                             keepdims=True)
    _store_transposed(v_ref, vt_sc)

    def load_q(hh, queries):
        return q_ref[0, queries, hh * HEAD_DIM:(hh + 1) * HEAD_DIM]

    def load_k(hh, keys):
        return k_ref[0, keys, hh * HEAD_DIM:(hh + 1) * HEAD_DIM]

    def load_vt(hh, keys):
        return vt_sc[hh * HEAD_DIM:(hh + 1) * HEAD_DIM, keys]

    def score_mod(hh, queries, keys, t):
        return t - col_sc[hh, keys, :]

    def query_shift(hh, queries):
        return crow_ref[0, pl.ds(2 * hp + hh, 1), queries]

    def emit(queries, results):
        o_t = jnp.concatenate([acc / l for acc, l in results], axis=0)
        o_ref[0, queries, :] = o_t.T.astype(bf16)

    _causal_flash_t(range(2), load_q, load_k, load_vt, score_mod, query_shift, emit)


def _fox_attention(qkv, cum_col, cum_row):
    B = qkv.shape[0]
    nblk = HALF // LANES
    first = 3 * nblk
    return pl.pallas_call(
        _fox_kernel,
        grid=(N_HEADS_D // 2, B),
        in_specs=[pl.BlockSpec((1, SEQ, LANES), lambda g, b: (b, 0, 0)),
                  pl.BlockSpec((1, N_HEADS_D, SEQ), lambda g, b: (b, 0, 0)),
                  _head_spec(first), _head_spec(first + nblk), _head_spec(first + 2 * nblk)],
        out_specs=_head_spec(0),
        out_shape=jax.ShapeDtypeStruct((B, SEQ, HALF), bf16),
        scratch_shapes=[pltpu.VMEM((2, SEQ, 1), f32), pltpu.VMEM((LANES, SEQ), bf16)],
        compiler_params=pltpu.CompilerParams(
            dimension_semantics=("arbitrary", "arbitrary"), vmem_limit_bytes=VMEM_LIMIT),
        name="fox_attention",
    )(cum_col, cum_row, qkv, qkv, qkv)


def _post_kernel(o1_ref, o2_ref, x_ref, g1_ref, sc2_ref, sh2_ref, g2_ref, ln_ref,
                 wo_ref, wi_ref, wout_ref, out_ref):
    y = _dot(o1_ref[0], wo_ref[0:HALF, :]) + _dot(o2_ref[0], wo_ref[HALF:, :])
    ln = ln_ref[...]
    x1 = _layer_norm(DEEPNORM_ALPHA * x_ref[0] + (1.0 + g1_ref[0]) * y, ln[0:1], ln[1:2])
    h = (x1 * (1.0 + sc2_ref[0]) + sh2_ref[0]).astype(bf16)
    y2 = jnp.zeros((ROW_TILE, D_MODEL), f32)
    for c in range(D_FF // FF_CHUNK):
        g = _dot(h, wi_ref[:, c * FF_CHUNK:(c + 1) * FF_CHUNK])
        u = _dot(h, wi_ref[:, D_FF + c * FF_CHUNK:D_FF + (c + 1) * FF_CHUNK])
        a = (g * jax.nn.sigmoid(g) * u).astype(bf16)
        y2 = y2 + _dot(a, wout_ref[c * FF_CHUNK:(c + 1) * FF_CHUNK, :])
    out_ref[0] = _layer_norm(DEEPNORM_ALPHA * x1 + (1.0 + g2_ref[0]) * y2, ln[2:3], ln[3:4])


def _post(o1, o2, x, g1, sc2, sh2, g2, ln, w_o, w_ffn_in, w_ffn_out):
    B = x.shape[0]
    return pl.pallas_call(
        _post_kernel,
        grid=(B, SEQ // ROW_TILE),
        in_specs=[_row_spec(HALF), _row_spec(HALF), _row_spec(D_MODEL),
                  _batch_vec_spec(), _batch_vec_spec(), _batch_vec_spec(), _batch_vec_spec(),
                  pl.BlockSpec((4, D_MODEL), lambda b, i: (0, 0)),
                  _resident(w_o.shape), _resident(w_ffn_in.shape), _resident(w_ffn_out.shape)],
        out_specs=_row_spec(D_MODEL),
        out_shape=jax.ShapeDtypeStruct((B, SEQ, D_MODEL), f32),
        compiler_params=pltpu.CompilerParams(
            dimension_semantics=("arbitrary", "arbitrary"), vmem_limit_bytes=VMEM_LIMIT),
        name="out_proj_ffn",
    )(o1, o2, x, g1, sc2, sh2, g2, ln, w_o, w_ffn_in, w_ffn_out)


def _scaled_in_weights(w):
    scale = np.ones((IN_MAIN,), np.float32)
    scale[0:HALF] = ATTN_SCALE
    scale[3 * HALF:4 * HALF] = ATTN_SCALE
    return (w[:, :IN_MAIN] * scale).astype(bf16)


def kernel(x, c, rel_bias, w_ada, b_ada, ln_g, ln_b, w_in_ab, diff_lambda, diff_subln_g,
           w_in_cd, forget_b, w_o, w_ffn_in, w_ffn_out):
    B = x.shape[0]
    ada = _ada(c, w_ada, b_ada)
    for l in range(DEPTH):
        sh1, sc1, g1, sh2, sc2, g2 = [a.reshape(B, 1, D_MODEL) for a in jnp.split(ada[l], 6, axis=-1)]
        i = l // 2
        if l % 2 == 0:
            lambda_init = 0.8 - 0.6 * math.exp(-0.3 * l)
            qkv = _in_proj(x, sc1, sh1, _scaled_in_weights(w_in_ab[i]))
            o1 = _diff_attention(qkv, rel_bias, diff_lambda[i], diff_subln_g[i], lambda_init)
            o2 = _dilated_attention(qkv, rel_bias)
        else:
            w_gate = jnp.zeros((D_MODEL, LANES), f32).at[:, :N_HEADS_D].set(w_in_cd[i][:, IN_MAIN:])
            qkv, fd = _in_proj(x, sc1, sh1, _scaled_in_weights(w_in_cd[i]), w_gate.astype(bf16))
            cum_col, cum_row = _forget_cum(fd, forget_b[i])
            o1 = _moba_attention(qkv, rel_bias)
            o2 = _fox_attention(qkv, cum_col, cum_row)
        ln = jnp.stack([ln_g[l, 0], ln_b[l, 0], ln_g[l, 1], ln_b[l, 1]])
        x = _post(o1, o2, x, g1, sc2, sh2, g2, ln, w_o[l].astype(bf16),
                  w_ffn_in[l].astype(bf16), w_ffn_out[l].astype(bf16))
    return x
```

```python
import functools
import math

import numpy as np
import jax
import jax.numpy as jnp
from jax import lax
from jax.experimental import pallas as pl
from jax.experimental.pallas import tpu as pltpu

f32 = jnp.float32
bf16 = jnp.bfloat16

D_MODEL = 1024
SEQ = 2048
DEPTH = 2
HEAD_DIM = 64
N_HEADS_A = 4
DV_A = 2 * HEAD_DIM
N_HEADS_B = 8
DILATED_PATTERNS = ((128, 1), (512, 4), (2048, 16))
DIL_BLOCK = 128
N_HEADS_C = 8
MOBA_BLOCK = 256
MOBA_TOPK = 3
N_MOBA_BLOCKS = SEQ // MOBA_BLOCK
N_HEADS_D = 8
NUM_BUCKETS = 32
MAX_DISTANCE = 128
D_FF = 2816
LN_EPS = 1e-5
DEEPNORM_ALPHA = (2 * DEPTH) ** 0.25
ATTN_SCALE = HEAD_DIM ** -0.5
HALF = D_MODEL // 2
IN_MAIN = 3 * D_MODEL

LANES = 128
ATT_TILE = 256
KEY_CHUNK = 256
N_ATT_TILES = SEQ // ATT_TILE
SUM_ROWS = 16
PIPELINE_DEPTH = 6
MXU_DIM = 256
FF_CHUNKS = ((0, 6 * MXU_DIM), (6 * MXU_DIM, D_FF))
ROW_TILE = 512
POST_ROW_TILE = 1024
ROW_SUB = 256
VMEM_LIMIT = 56 * 1024 * 1024

NEG_INF = float("-inf")


def _t5_bucket_starts():
    n = np.arange(0, 4 * SEQ, dtype=np.int64)
    max_exact = NUM_BUCKETS // 2
    nf = np.maximum(n, 1).astype(np.float32)
    val = (np.log(nf / np.float32(max_exact)) / np.float32(math.log(MAX_DISTANCE / max_exact))
           * np.float32(NUM_BUCKETS - max_exact))
    large = np.minimum(max_exact + val.astype(np.int32), NUM_BUCKETS - 1)
    bucket = np.where(n < max_exact, n, large)
    assert np.all(np.diff(bucket) >= 0)
    return tuple(int(np.argmax(bucket >= b)) for b in range(NUM_BUCKETS))


T5_STARTS = _t5_bucket_starts()
assert T5_STARTS[-1] <= ATT_TILE + 1


def _t5_bias(dist, tbl_ref, col):
    out = jnp.full(dist.shape, tbl_ref[0, col], f32)
    for b in range(1, NUM_BUCKETS):
        out = jnp.where(dist >= T5_STARTS[b], tbl_ref[b, col], out)
    return out


def _dot_nt(a, b):
    return lax.dot_general(a, b, (((1,), (1,)), ((), ())), preferred_element_type=f32)


def _dot(a, b):
    return jnp.dot(a, b, preferred_element_type=f32)


def _software_pipeline(items, first_stage, second_stage):
    pending = [first_stage(it) for it in items[:PIPELINE_DEPTH]]
    for idx, it in enumerate(items):
        cur = pending.pop(0)
        if idx + PIPELINE_DEPTH < len(items):
            pending.append(first_stage(items[idx + PIPELINE_DEPTH]))
        second_stage(it, cur)


def _store_transposed(v_ref, vt_ref, group):
    ones = jnp.ones((SUM_ROWS, ATT_TILE), bf16)
    for c in range(N_ATT_TILES):
        rows = slice(c * ATT_TILE, (c + 1) * ATT_TILE)
        vt = v_ref[0, rows, :].astype(f32).T.astype(bf16)
        for g in range(LANES // group):
            base = g * (group + SUM_ROWS)
            vt_ref[base:base + group, rows] = vt[g * group:(g + 1) * group, :]
            vt_ref[base + group:base + group + SUM_ROWS, rows] = ones


def _vt_rows(g, group):
    return slice(g * (group + SUM_ROWS), (g + 1) * (group + SUM_ROWS))


def _causal_flash_t(streams, load_q, load_k, load_vt, score_mod, query_shift, emit):
    streams = list(streams)
    krow = lax.broadcasted_iota(jnp.int32, (KEY_CHUNK, ATT_TILE), 0)
    qcol = lax.broadcasted_iota(jnp.int32, (KEY_CHUNK, ATT_TILE), 1)
    per_tile = ATT_TILE // KEY_CHUNK
    state = {}

    def spans(work):
        _, i, c = work
        return (slice(i * ATT_TILE, (i + 1) * ATT_TILE), slice(c * KEY_CHUNK, (c + 1) * KEY_CHUNK))

    def first_stage(work):
        s, i, c = work
        queries, keys = spans(work)
        t = score_mod(s, queries, keys, _dot_nt(load_k(s, keys), load_q(s, queries)))
        if keys.stop - 1 > queries.start:
            t = jnp.where(krow + (keys.start - queries.start) <= qcol, t, NEG_INF)
        return t, jnp.max(t, axis=0, keepdims=True)

    def second_stage(work, staged):
        s, i, c = work
        queries, keys = spans(work)
        t, tmax = staged
        m, l, acc = state.get((s, i), (None, None, None))
        m_new = tmax if m is None else jnp.maximum(m, tmax)
        shift = query_shift(s, queries)
        if shift is None:
            p = jnp.exp(t - m_new)
        else:
            p = jnp.exp(t + (shift - (m_new + shift)))
        pv = _dot(load_vt(s, keys), p.astype(bf16))
        dv = pv.shape[0] - SUM_ROWS
        pv, psum = pv[:dv], pv[dv:dv + 1]
        if m is None:
            l, acc = psum, pv
        else:
            alpha = jnp.exp(m - m_new)
            l, acc = alpha * l + psum, alpha * acc + pv
        state[(s, i)] = (m_new, l, acc)
        if (s, c) == last_of_tile[i]:
            emit(queries, [(state[(s2, i)][2], state[(s2, i)][1]) for s2 in streams])

    work, last_of_tile = [], {}
    for i in reversed(range(N_ATT_TILES)):
        first = i * per_tile
        order = [first] + [c for c in range((i + 1) * per_tile) if c != first]
        work += [(s, i, c) for c in order for s in streams]
        last_of_tile[i] = (streams[-1], order[-1])
    _software_pipeline(work, first_stage, second_stage)


def _tile_iotas(n):
    row = lax.broadcasted_iota(jnp.int32, (n, n), 0)
    col = lax.broadcasted_iota(jnp.int32, (n, n), 1)
    return row, col


def _layer_norm(x, g, b):
    mu = jnp.mean(x, axis=1, keepdims=True)
    xc = x - mu
    var = jnp.mean(xc * xc, axis=1, keepdims=True)
    return xc * lax.rsqrt(var + LN_EPS) * g + b


def _ada_kernel(c_ref, w_ref, b_ref, o_ref):
    c = c_ref[...]
    cs = c * jax.nn.sigmoid(c)
    o_ref[0] = _dot(cs.astype(bf16), w_ref[0].astype(bf16)) + b_ref[0]


def _ada(c, w_ada, b_ada):
    B = c.shape[0]
    tn = 1536
    return pl.pallas_call(
        _ada_kernel,
        grid=(DEPTH, 6 * D_MODEL // tn),
        in_specs=[
            pl.BlockSpec((B, D_MODEL), lambda l, j: (0, 0)),
            pl.BlockSpec((1, D_MODEL, tn), lambda l, j: (l, 0, j)),
            pl.BlockSpec((1, 1, tn), lambda l, j: (l, 0, j)),
        ],
        out_specs=pl.BlockSpec((1, B, tn), lambda l, j: (l, 0, j)),
        out_shape=jax.ShapeDtypeStruct((DEPTH, B, 6 * D_MODEL), f32),
        compiler_params=pltpu.CompilerParams(
            dimension_semantics=("arbitrary", "arbitrary"), vmem_limit_bytes=VMEM_LIMIT),
        name="ada",
    )(c, w_ada, b_ada.reshape(DEPTH, 1, 6 * D_MODEL))


def _proj_kernel(x_ref, sc_ref, sh_ref, w_ref, o_ref):
    h = (x_ref[0] * (1.0 + sc_ref[0]) + sh_ref[0]).astype(bf16)
    o_ref[0] = _dot(h, w_ref[...]).astype(bf16)


def _proj_gate_kernel(x_ref, sc_ref, sh_ref, w_ref, wf_ref, o_ref, f_ref):
    h = (x_ref[0] * (1.0 + sc_ref[0]) + sh_ref[0]).astype(bf16)
    o_ref[0] = _dot(h, w_ref[...]).astype(bf16)
    f_ref[0] = _dot(h, wf_ref[...])


def _resident(shape):
    zeros = (0,) * len(shape)
    return pl.BlockSpec(shape, lambda *_: zeros, pipeline_mode=pl.Buffered(1))


def _row_spec(width, rows=ROW_TILE):
    return pl.BlockSpec((1, rows, width), lambda b, i: (b, i, 0))


def _batch_vec_spec():
    return pl.BlockSpec((1, 1, D_MODEL), lambda b, i: (b, 0, 0))


def _in_proj(x, sc, sh, w, w_gate=None):
    B = x.shape[0]
    in_specs = [_row_spec(D_MODEL), _batch_vec_spec(), _batch_vec_spec(), _resident(w.shape)]
    out_specs = [_row_spec(IN_MAIN)]
    out_shape = [jax.ShapeDtypeStruct((B, SEQ, IN_MAIN), bf16)]
    args = [x, sc, sh, w]
    kern = _proj_kernel
    if w_gate is not None:
        in_specs.append(_resident(w_gate.shape))
        out_specs.append(_row_spec(LANES))
        out_shape.append(jax.ShapeDtypeStruct((B, SEQ, LANES), f32))
        args.append(w_gate)
        kern = _proj_gate_kernel
    res = pl.pallas_call(
        kern,
        grid=(B, SEQ // ROW_TILE),
        in_specs=in_specs,
        out_specs=out_specs,
        out_shape=out_shape,
        compiler_params=pltpu.CompilerParams(
            dimension_semantics=("arbitrary", "arbitrary"), vmem_limit_bytes=VMEM_LIMIT),
        name="in_proj",
    )(*args)
    return res if w_gate is not None else res[0]


def _near_bias_t(tbl_ref, head):
    kk = lax.broadcasted_iota(jnp.int32, (2 * ATT_TILE, ATT_TILE), 0)
    qq = lax.broadcasted_iota(jnp.int32, (2 * ATT_TILE, ATT_TILE), 1)
    return _t5_bias(qq - kk + ATT_TILE, tbl_ref, head) - tbl_ref[NUM_BUCKETS - 1, head]


def _add_near_bias(t, bias_ref, lead, queries, keys):
    rel = keys.start - (queries.start - ATT_TILE)
    if rel < 0:
        return t
    return t + bias_ref[lead + (slice(rel, rel + KEY_CHUNK), slice(None))]


def _smem_spec():
    return pl.BlockSpec(memory_space=pltpu.SMEM)


def _head_spec(first_block):
    return pl.BlockSpec((1, SEQ, LANES), lambda g, b: (b, 0, first_block + g))


def _diff_kernel(tbl_ref, lam_ref, g_ref, q_ref, k_ref, v_ref, o_ref, bias_sc, vt_sc, *,
                 lambda_init):
    h = pl.program_id(0)

    @pl.when(pl.program_id(1) == 0)
    def _():
        bias_sc[...] = _near_bias_t(tbl_ref, h)

    _store_transposed(v_ref, vt_sc, DV_A)
    lm = lam_ref[...]
    lam_full = (jnp.exp(jnp.sum(lm[0:1] * lm[1:2], axis=1, keepdims=True))
                - jnp.exp(jnp.sum(lm[2:3] * lm[3:4], axis=1, keepdims=True)) + lambda_init)

    def load_q(m, queries):
        return q_ref[0, queries, m * HEAD_DIM:(m + 1) * HEAD_DIM]

    def load_k(m, keys):
        return k_ref[0, keys, m * HEAD_DIM:(m + 1) * HEAD_DIM]

    def load_vt(m, keys):
        return vt_sc[:, keys]

    def score_mod(m, queries, keys, t):
        return _add_near_bias(t, bias_sc, (), queries, keys)

    def emit(queries, results):
        (acc1, l1), (acc2, l2) = results
        o = (acc1 / l1 - lam_full * (acc2 / l2)).T
        o = o * lax.rsqrt(jnp.mean(o * o, axis=1, keepdims=True) + LN_EPS)
        o_ref[0, queries, :] = (o * g_ref[...] * (1.0 - lambda_init)).astype(bf16)

    _causal_flash_t(range(2), load_q, load_k, load_vt, score_mod, lambda m, queries: None, emit)


def _diff_attention(qkv, table, lam, subln_g, lambda_init):
    B = qkv.shape[0]
    nblk = HALF // LANES
    return pl.pallas_call(
        functools.partial(_diff_kernel, lambda_init=lambda_init),
        grid=(N_HEADS_A, B),
        in_specs=[
            _smem_spec(),
            pl.BlockSpec((4, HEAD_DIM), lambda g, b: (0, 0)),
            pl.BlockSpec((1, DV_A), lambda g, b: (0, 0)),
            _head_spec(0), _head_spec(nblk), _head_spec(2 * nblk),
        ],
        out_specs=_head_spec(0),
        out_shape=jax.ShapeDtypeStruct((B, SEQ, HALF), bf16),
        scratch_shapes=[pltpu.VMEM((2 * ATT_TILE, ATT_TILE), f32),
                        pltpu.VMEM((DV_A + SUM_ROWS, SEQ), bf16)],
        compiler_params=pltpu.CompilerParams(
            dimension_semantics=("arbitrary", "arbitrary"), vmem_limit_bytes=VMEM_LIMIT),
        name="diff_attention",
    )(table, lam, subln_g.reshape(1, DV_A), qkv, qkv, qkv)


def _dil_kernel(tbl_ref, q_ref, k_ref, v_ref, o_ref, qf, kf, vf, bias_sc, osc, lsc):
    hp = pl.program_id(0)
    W = DIL_BLOCK
    npat = len(DILATED_PATTERNS)
    row = lax.broadcasted_iota(jnp.int32, (W, 2 * W), 0)
    col = lax.broadcasted_iota(jnp.int32, (W, 2 * W), 1)
    du = row - col + W

    @pl.when(pl.program_id(1) == 0)
    def _():
        for p, (_, dil) in enumerate(DILATED_PATTERNS):
            for hh in range(2):
                bias = _t5_bias(du * dil, tbl_ref, N_HEADS_A + 2 * hp + hh)
                bias_sc[p, hh] = jnp.where((du >= 0) & (du <= W), bias, NEG_INF)

    qf[...] = q_ref[0].astype(f32)
    kf[...] = k_ref[0].astype(f32)
    vf[...] = v_ref[0].astype(f32)
    lane = lax.broadcasted_iota(jnp.int32, (W, LANES), 1)
    head_lanes = [lane < HEAD_DIM, lane >= HEAD_DIM]
    blocks, staged_out = {}, {}

    def block(p, r, n):
        if (p, r, n) not in blocks:
            dil = DILATED_PATTERNS[p][1]
            rows = pl.ds(r + n * W * dil, W, stride=dil)
            blocks[(p, r, n)] = tuple(ref[rows, :].astype(bf16) for ref in (qf, kf, vf))
        return blocks[(p, r, n)]

    def scores(item):
        p, r, n, hh = item
        q, k, _ = block(p, r, n)
        q = jnp.where(head_lanes[hh], q, jnp.zeros_like(q))
        if n > 0:
            k = jnp.concatenate([block(p, r, n - 1)[1], k], axis=0)
            s = _dot_nt(q, k) + bias_sc[p, hh]
        else:
            s = _dot_nt(q, k) + bias_sc[p, hh, :, W:]
        return s, jnp.max(s, axis=1, keepdims=True)

    def finish(item, staged):
        p, r, n, hh = item
        s, m = staged
        v = block(p, r, n)[2]
        if n > 0:
            v = jnp.concatenate([block(p, r, n - 1)[2], v], axis=0)
        e = jnp.exp(s - m)
        den = jnp.sum(e, axis=1, keepdims=True)
        num = _dot(e.astype(bf16), v)
        staged_out[hh] = (num / den, m + jnp.log(den))
        if hh == 1:
            dil = DILATED_PATTERNS[p][1]
            rows = pl.ds(r + n * W * dil, W, stride=dil)
            (o0, l0), (o1, l1) = staged_out[0], staged_out[1]
            osc[p, rows, :] = jnp.where(head_lanes[0], o0, o1)
            lsc[p, rows, :] = jnp.where(head_lanes[0], l0, l1)

    items = []
    for p, (window, dil) in enumerate(DILATED_PATTERNS):
        assert window // dil == W
        nb = SEQ // dil // W
        items += [(p, r, n, hh) for r in range(dil) for n in range(nb) for hh in range(2)]
    _software_pipeline(items, scores, finish)

    def merge_body(i, _):
        rows = pl.ds(pl.multiple_of(i * ATT_TILE, ATT_TILE), ATT_TILE)
        ls = [lsc[p, rows, :] for p in range(npat)]
        mx = functools.reduce(jnp.maximum, ls)
        ws = [jnp.exp(l - mx) for l in ls]
        tot = functools.reduce(lambda a, b: a + b, ws)
        out = functools.reduce(lambda a, b: a + b,
                               [(w / tot) * osc[p, rows, :] for p, w in enumerate(ws)])
        o_ref[0, rows, :] = out.astype(bf16)
        return 0

    lax.fori_loop(0, N_ATT_TILES, merge_body, 0)


def _dilated_attention(qkv, table):
    B = qkv.shape[0]
    nblk = HALF // LANES
    first = 3 * nblk
    npat = len(DILATED_PATTERNS)
    return pl.pallas_call(
        _dil_kernel,
        grid=(N_HEADS_B // 2, B),
        in_specs=[_smem_spec(), _head_spec(first), _head_spec(first + nblk),
                  _head_spec(first + 2 * nblk)],
        out_specs=_head_spec(0),
        out_shape=jax.ShapeDtypeStruct((B, SEQ, HALF), bf16),
        scratch_shapes=[
            pltpu.VMEM((SEQ, LANES), f32), pltpu.VMEM((SEQ, LANES), f32), pltpu.VMEM((SEQ, LANES), f32),
            pltpu.VMEM((npat, 2, DIL_BLOCK, 2 * DIL_BLOCK), f32),
            pltpu.VMEM((npat, SEQ, LANES), f32), pltpu.VMEM((npat, SEQ, LANES), f32),
        ],
        compiler_params=pltpu.CompilerParams(
            dimension_semantics=("arbitrary", "arbitrary"), vmem_limit_bytes=VMEM_LIMIT),
        name="dilated_attention",
    )(table, qkv, qkv, qkv)


def _moba_kernel(tbl_ref, q_ref, k_ref, v_ref, o_ref, bias_sc, vt_sc):
    hp = pl.program_id(0)
    assert MOBA_BLOCK == ATT_TILE == KEY_CHUNK

    @pl.when(pl.program_id(1) == 0)
    def _():
        for hh in range(2):
            bias_sc[hh] = _near_bias_t(tbl_ref, 2 * hp + hh)

    _store_transposed(v_ref, vt_sc, HEAD_DIM)

    def load_q(hh, queries):
        return q_ref[0, queries, hh * HEAD_DIM:(hh + 1) * HEAD_DIM]

    def load_k(hh, keys):
        return k_ref[0, keys, hh * HEAD_DIM:(hh + 1) * HEAD_DIM]

    def load_vt(hh, keys):
        return vt_sc[_vt_rows(hh, HEAD_DIM), keys]

    k_means = []
    for hh in range(2):
        means = [jnp.sum(load_k(hh, slice(n * MOBA_BLOCK, (n + 1) * MOBA_BLOCK)).astype(f32), axis=0,
                         keepdims=True) * (1.0 / MOBA_BLOCK) for n in range(N_MOBA_BLOCKS)]
        k_means.append(jnp.concatenate(means + [jnp.zeros_like(m) for m in means], axis=0).astype(bf16))

    block_masks = {}

    def block_mask(hh, queries, blk):
        own = queries.start // MOBA_BLOCK
        if own <= MOBA_TOPK:
            return None
        if (hh, own) not in block_masks:
            gate = _dot_nt(k_means[hh], load_q(hh, queries))
            g = [gate[n:n + 1, :] for n in range(own)]
            masks = []
            for n in range(own):
                rank = jnp.zeros((1, ATT_TILE), jnp.int32)
                for m in range(own):
                    if m != n:
                        beats = (g[m] >= g[n]) if m < n else (g[m] > g[n])
                        rank = rank + beats.astype(jnp.int32)
                masks.append(jnp.where(rank < MOBA_TOPK, 0.0, NEG_INF))
            block_masks[(hh, own)] = masks
        return block_masks[(hh, own)][blk]

    def score_mod(hh, queries, keys, t):
        t = _add_near_bias(t, bias_sc, (hh,), queries, keys)
        if keys.start < queries.start:
            mask = block_mask(hh, queries, keys.start // MOBA_BLOCK)
            if mask is not None:
                t = t + mask
        return t

    def emit(queries, results):
        o_t = jnp.concatenate([acc / l for acc, l in results], axis=0)
        o_ref[0, queries, :] = o_t.T.astype(bf16)

    _causal_flash_t(range(2), load_q, load_k, load_vt, score_mod, lambda hh, queries: None, emit)


def _moba_attention(qkv, table):
    B = qkv.shape[0]
    nblk = HALF // LANES
    return pl.pallas_call(
        _moba_kernel,
        grid=(N_HEADS_C // 2, B),
        in_specs=[_smem_spec(), _head_spec(0), _head_spec(nblk), _head_spec(2 * nblk)],
        out_specs=_head_spec(0),
        out_shape=jax.ShapeDtypeStruct((B, SEQ, HALF), bf16),
        scratch_shapes=[pltpu.VMEM((2, 2 * ATT_TILE, ATT_TILE), f32),
                        pltpu.VMEM((2 * (HEAD_DIM + SUM_ROWS), SEQ), bf16)],
        compiler_params=pltpu.CompilerParams(
            dimension_semantics=("arbitrary", "arbitrary"), vmem_limit_bytes=VMEM_LIMIT),
        name="moba_attention",
    )(table, qkv, qkv, qkv)


def _forget_cum_kernel(fd_ref, fb_ref, col_ref, row_ref):
    x = jax.nn.log_sigmoid(fd_ref[0] + fb_ref[...])
    t = lax.broadcasted_iota(jnp.int32, (SEQ, LANES), 0)
    shift = 1
    while shift < SEQ:
        x = x + jnp.where(t >= shift, pltpu.roll(x, shift, axis=0), 0.0)
        shift *= 2
    col_ref[0] = x
    for c in range(SEQ // LANES):
        blk = x[c * LANES:(c + 1) * LANES, :].T
        row_ref[0, :, c * LANES:(c + 1) * LANES] = blk[0:N_HEADS_D, :]


def _forget_cum(fd, forget_b):
    B = fd.shape[0]
    fb = jnp.zeros((1, LANES), f32).at[0, :N_HEADS_D].set(forget_b)
    return pl.pallas_call(
        _forget_cum_kernel,
        grid=(B,),
        in_specs=[pl.BlockSpec((1, SEQ, LANES), lambda b: (b, 0, 0)),
                  pl.BlockSpec((1, LANES), lambda b: (0, 0))],
        out_specs=[pl.BlockSpec((1, SEQ, LANES), lambda b: (b, 0, 0)),
                   pl.BlockSpec((1, N_HEADS_D, SEQ), lambda b: (b, 0, 0))],
        out_shape=[jax.ShapeDtypeStruct((B, SEQ, LANES), f32),
                   jax.ShapeDtypeStruct((B, N_HEADS_D, SEQ), f32)],
        compiler_params=pltpu.CompilerParams(
            dimension_semantics=("arbitrary",), vmem_limit_bytes=VMEM_LIMIT),
        name="forget_cum",
    )(fd, fb)


def _fox_kernel(ccol_ref, crow_ref, q_ref, k_ref, v_ref, o_ref, col_sc, vt_sc):
    hp = pl.program_id(0)
    lane = lax.broadcasted_iota(jnp.int32, (SEQ, LANES), 1)
    for hh in range(2):
        col_sc[hh] = jnp.sum(jnp.where(lane == 2 * hp + hh, ccol_ref[0], 0.0), axis=1,
                             keepdims=True)
    _store_transposed(v_ref, vt_sc, HEAD_DIM)

    def load_q(hh, queries):
        return q_ref[0, queries, hh * HEAD_DIM:(hh + 1) * HEAD_DIM]

    def load_k(hh, keys):
        return k_ref[0, keys, hh * HEAD_DIM:(hh + 1) * HEAD_DIM]

    def load_vt(hh, keys):
        return vt_sc[_vt_rows(hh, HEAD_DIM), keys]

    def score_mod(hh, queries, keys, t):
        return t - col_sc[hh, keys, :]

    def query_shift(hh, queries):
        return crow_ref[0, pl.ds(2 * hp + hh, 1), queries]

    def emit(queries, results):
        o_t = jnp.concatenate([acc / l for acc, l in results], axis=0)
        o_ref[0, queries, :] = o_t.T.astype(bf16)

    _causal_flash_t(range(2), load_q, load_k, load_vt, score_mod, query_shift, emit)


def _fox_attention(qkv, cum_col, cum_row):
    B = qkv.shape[0]
    nblk = HALF // LANES
    first = 3 * nblk
    return pl.pallas_call(
        _fox_kernel,
        grid=(N_HEADS_D // 2, B),
        in_specs=[pl.BlockSpec((1, SEQ, LANES), lambda g, b: (b, 0, 0)),
                  pl.BlockSpec((1, N_HEADS_D, SEQ), lambda g, b: (b, 0, 0)),
                  _head_spec(first), _head_spec(first + nblk), _head_spec(first + 2 * nblk)],
        out_specs=_head_spec(0),
        out_shape=jax.ShapeDtypeStruct((B, SEQ, HALF), bf16),
        scratch_shapes=[pltpu.VMEM((2, SEQ, 1), f32), pltpu.VMEM((2 * (HEAD_DIM + SUM_ROWS), SEQ), bf16)],
        compiler_params=pltpu.CompilerParams(
            dimension_semantics=("arbitrary", "arbitrary"), vmem_limit_bytes=VMEM_LIMIT),
        name="fox_attention",
    )(cum_col, cum_row, qkv, qkv, qkv)


def _post_kernel(o1_ref, o2_ref, x_ref, g1_ref, sc2_ref, sh2_ref, g2_ref, ln_ref,
                 wo_ref, wi_ref, wout_ref, out_ref):
    ln = ln_ref[...]
    n_sub = POST_ROW_TILE // ROW_SUB

    def mix(r):
        rows = slice(r * ROW_SUB, (r + 1) * ROW_SUB)
        y = _dot(o1_ref[0, rows, :], wo_ref[0:HALF, :]) + _dot(o2_ref[0, rows, :], wo_ref[HALF:, :])
        x1 = _layer_norm(DEEPNORM_ALPHA * x_ref[0, rows, :] + (1.0 + g1_ref[0]) * y, ln[0:1], ln[1:2])
        return x1, (x1 * (1.0 + sc2_ref[0]) + sh2_ref[0]).astype(bf16)

    def ffn_chunk(h, chunk):
        c0, c1 = chunk
        g = _dot(h, wi_ref[:, c0:c1])
        u = _dot(h, wi_ref[:, D_FF + c0:D_FF + c1])
        return _dot((g * jax.nn.sigmoid(g) * u).astype(bf16), wout_ref[c0:c1, :])

    def finish(r, x1, y2):
        rows = slice(r * ROW_SUB, (r + 1) * ROW_SUB)
        out_ref[0, rows, :] = _layer_norm(DEEPNORM_ALPHA * x1 + (1.0 + g2_ref[0]) * y2,
                                          ln[2:3], ln[3:4])

    staged, done = mix(0), None
    for r in range(n_sub):
        x1, h = staged
        y2 = ffn_chunk(h, FF_CHUNKS[0])
        if done is not None:
            finish(*done)
        if r + 1 < n_sub:
            staged = mix(r + 1)
        done = (r, x1, y2 + ffn_chunk(h, FF_CHUNKS[1]))
    finish(*done)


def _post(o1, o2, x, g1, sc2, sh2, g2, ln, w_o, w_ffn_in, w_ffn_out):
    B = x.shape[0]
    return pl.pallas_call(
        _post_kernel,
        grid=(B, SEQ // POST_ROW_TILE),
        in_specs=[_row_spec(HALF, POST_ROW_TILE), _row_spec(HALF, POST_ROW_TILE),
                  _row_spec(D_MODEL, POST_ROW_TILE),
                  _batch_vec_spec(), _batch_vec_spec(), _batch_vec_spec(), _batch_vec_spec(),
                  pl.BlockSpec((4, D_MODEL), lambda b, i: (0, 0)),
                  _resident(w_o.shape), _resident(w_ffn_in.shape), _resident(w_ffn_out.shape)],
        out_specs=_row_spec(D_MODEL, POST_ROW_TILE),
        out_shape=jax.ShapeDtypeStruct((B, SEQ, D_MODEL), f32),
        compiler_params=pltpu.CompilerParams(
            dimension_semantics=("arbitrary", "arbitrary"), vmem_limit_bytes=VMEM_LIMIT),
        name="out_proj_ffn",
    )(o1, o2, x, g1, sc2, sh2, g2, ln, w_o, w_ffn_in, w_ffn_out)


def _scaled_in_weights(w):
    scale = np.ones((IN_MAIN,), np.float32)
    scale[0:HALF] = ATTN_SCALE
    scale[3 * HALF:4 * HALF] = ATTN_SCALE
    return (w[:, :IN_MAIN] * scale).astype(bf16)


def kernel(x, c, rel_bias, w_ada, b_ada, ln_g, ln_b, w_in_ab, diff_lambda, diff_subln_g,
           w_in_cd, forget_b, w_o, w_ffn_in, w_ffn_out):
    B = x.shape[0]
    ada = _ada(c, w_ada, b_ada)
    for l in range(DEPTH):
        sh1, sc1, g1, sh2, sc2, g2 = [a.reshape(B, 1, D_MODEL) for a in jnp.split(ada[l], 6, axis=-1)]
        i = l // 2
        if l % 2 == 0:
            lambda_init = 0.8 - 0.6 * math.exp(-0.3 * l)
            qkv = _in_proj(x, sc1, sh1, _scaled_in_weights(w_in_ab[i]))
            o1 = _diff_attention(qkv, rel_bias, diff_lambda[i], diff_subln_g[i], lambda_init)
            o2 = _dilated_attention(qkv, rel_bias)
        else:
            w_gate = jnp.zeros((D_MODEL, LANES), f32).at[:, :N_HEADS_D].set(w_in_cd[i][:, IN_MAIN:])
            qkv, fd = _in_proj(x, sc1, sh1, _scaled_in_weights(w_in_cd[i]), w_gate.astype(bf16))
            cum_col, cum_row = _forget_cum(fd, forget_b[i])
            o1 = _moba_attention(qkv, rel_bias)
            o2 = _fox_attention(qkv, cum_col, cum_row)
        ln = jnp.stack([ln_g[l, 0], ln_b[l, 0], ln_g[l, 1], ln_b[l, 1]])
        x = _post(o1, o2, x, g1, sc2, sh2, g2, ln, w_o[l].astype(bf16),
                  w_ffn_in[l].astype(bf16), w_ffn_out[l].astype(bf16))
    return x
```

```python
import functools
import math

import numpy as np
import jax
import jax.numpy as jnp
from jax import lax
from jax.experimental import pallas as pl
from jax.experimental.pallas import tpu as pltpu

f32 = jnp.float32
bf16 = jnp.bfloat16

D_MODEL = 1024
SEQ = 2048
DEPTH = 2
HEAD_DIM = 64
N_HEADS_A = 4
DV_A = 2 * HEAD_DIM
N_HEADS_B = 8
DILATED_PATTERNS = ((128, 1), (512, 4), (2048, 16))
DIL_BLOCK = 128
N_HEADS_C = 8
MOBA_BLOCK = 256
MOBA_TOPK = 3
N_MOBA_BLOCKS = SEQ // MOBA_BLOCK
N_HEADS_D = 8
NUM_BUCKETS = 32
MAX_DISTANCE = 128
D_FF = 2816
LN_EPS = 1e-5
DEEPNORM_ALPHA = (2 * DEPTH) ** 0.25
ATTN_SCALE = HEAD_DIM ** -0.5
LOG2_E = math.log2(math.e)
HALF = D_MODEL // 2
IN_MAIN = 3 * D_MODEL

LANES = 128
ATT_TILE = 256
KEY_CHUNK = 256
N_ATT_TILES = SEQ // ATT_TILE
SUM_ROWS = 16
PIPELINE_DEPTH = 6
MXU_DIM = 256
FF_CHUNKS = ((0, 6 * MXU_DIM), (6 * MXU_DIM, D_FF))
ROW_TILE = 512
POST_ROW_TILE = 1024
ROW_SUB = 256
VMEM_LIMIT = 56 * 1024 * 1024

NEG_INF = float("-inf")


def _t5_bucket_starts():
    n = np.arange(0, 4 * SEQ, dtype=np.int64)
    max_exact = NUM_BUCKETS // 2
    nf = np.maximum(n, 1).astype(np.float32)
    val = (np.log(nf / np.float32(max_exact)) / np.float32(math.log(MAX_DISTANCE / max_exact))
           * np.float32(NUM_BUCKETS - max_exact))
    large = np.minimum(max_exact + val.astype(np.int32), NUM_BUCKETS - 1)
    bucket = np.where(n < max_exact, n, large)
    assert np.all(np.diff(bucket) >= 0)
    return tuple(int(np.argmax(bucket >= b)) for b in range(NUM_BUCKETS))


T5_STARTS = _t5_bucket_starts()
assert T5_STARTS[-1] <= ATT_TILE + 1


def _t5_bias(dist, tbl_ref, col):
    out = jnp.full(dist.shape, tbl_ref[0, col], f32)
    for b in range(1, NUM_BUCKETS):
        out = jnp.where(dist >= T5_STARTS[b], tbl_ref[b, col], out)
    return out


def _dot_nt(a, b):
    return lax.dot_general(a, b, (((1,), (1,)), ((), ())), preferred_element_type=f32)


def _dot(a, b):
    return jnp.dot(a, b, preferred_element_type=f32)


def _software_pipeline(items, first_stage, second_stage):
    pending = [first_stage(it) for it in items[:PIPELINE_DEPTH]]
    for idx, it in enumerate(items):
        cur = pending.pop(0)
        if idx + PIPELINE_DEPTH < len(items):
            pending.append(first_stage(items[idx + PIPELINE_DEPTH]))
        second_stage(it, cur)


def _store_transposed(v_ref, vt_ref, group):
    ones = jnp.ones((SUM_ROWS, ATT_TILE), bf16)
    for c in range(N_ATT_TILES):
        rows = slice(c * ATT_TILE, (c + 1) * ATT_TILE)
        vt = v_ref[0, rows, :].astype(f32).T.astype(bf16)
        for g in range(LANES // group):
            base = g * (group + SUM_ROWS)
            vt_ref[base:base + group, rows] = vt[g * group:(g + 1) * group, :]
            vt_ref[base + group:base + group + SUM_ROWS, rows] = ones


def _vt_rows(g, group):
    return slice(g * (group + SUM_ROWS), (g + 1) * (group + SUM_ROWS))


def _causal_flash_t(streams, load_q, load_k, load_vt, score_mod, query_shift, emit):
    streams = list(streams)
    krow = lax.broadcasted_iota(jnp.int32, (KEY_CHUNK, ATT_TILE), 0)
    qcol = lax.broadcasted_iota(jnp.int32, (KEY_CHUNK, ATT_TILE), 1)
    per_tile = ATT_TILE // KEY_CHUNK
    state = {}

    def spans(work):
        _, i, c = work
        return (slice(i * ATT_TILE, (i + 1) * ATT_TILE), slice(c * KEY_CHUNK, (c + 1) * KEY_CHUNK))

    def first_stage(work):
        s, i, c = work
        queries, keys = spans(work)
        t = score_mod(s, queries, keys, _dot_nt(load_k(s, keys), load_q(s, queries)))
        if keys.stop - 1 > queries.start:
            t = jnp.where(krow + (keys.start - queries.start) <= qcol, t, NEG_INF)
        return t, jnp.max(t, axis=0, keepdims=True)

    def second_stage(work, staged):
        s, i, c = work
        queries, keys = spans(work)
        t, tmax = staged
        m, l, acc = state.get((s, i), (None, None, None))
        m_new = tmax if m is None else jnp.maximum(m, tmax)
        shift = query_shift(s, queries)
        if shift is None:
            p = jnp.exp2(t - m_new)
        else:
            p = jnp.exp2(t + (shift - (m_new + shift)))
        pv = _dot(load_vt(s, keys), p.astype(bf16))
        dv = pv.shape[0] - SUM_ROWS
        pv, psum = pv[:dv], pv[dv:dv + 1]
        if m is None:
            l, acc = psum, pv
        else:
            alpha = jnp.exp2(m - m_new)
            l, acc = alpha * l + psum, alpha * acc + pv
        state[(s, i)] = (m_new, l, acc)
        if (s, c) == last_of_tile[i]:
            emit(queries, [(state[(s2, i)][2], state[(s2, i)][1]) for s2 in streams])

    work, last_of_tile = [], {}
    for i in reversed(range(N_ATT_TILES)):
        first = i * per_tile
        order = [first] + [c for c in range((i + 1) * per_tile) if c != first]
        work += [(s, i, c) for c in order for s in streams]
        last_of_tile[i] = (streams[-1], order[-1])
    _software_pipeline(work, first_stage, second_stage)


def _tile_iotas(n):
    row = lax.broadcasted_iota(jnp.int32, (n, n), 0)
    col = lax.broadcasted_iota(jnp.int32, (n, n), 1)
    return row, col


def _layer_norm(x, g, b):
    mu = jnp.mean(x, axis=1, keepdims=True)
    xc = x - mu
    var = jnp.mean(xc * xc, axis=1, keepdims=True)
    return xc * lax.rsqrt(var + LN_EPS) * g + b


def _ada_kernel(c_ref, w_ref, b_ref, o_ref):
    c = c_ref[...]
    cs = c * jax.nn.sigmoid(c)
    o_ref[0] = _dot(cs.astype(bf16), w_ref[0].astype(bf16)) + b_ref[0]


def _ada(c, w_ada, b_ada):
    B = c.shape[0]
    tn = 1536
    return pl.pallas_call(
        _ada_kernel,
        grid=(DEPTH, 6 * D_MODEL // tn),
        in_specs=[
            pl.BlockSpec((B, D_MODEL), lambda l, j: (0, 0)),
            pl.BlockSpec((1, D_MODEL, tn), lambda l, j: (l, 0, j)),
            pl.BlockSpec((1, 1, tn), lambda l, j: (l, 0, j)),
        ],
        out_specs=pl.BlockSpec((1, B, tn), lambda l, j: (l, 0, j)),
        out_shape=jax.ShapeDtypeStruct((DEPTH, B, 6 * D_MODEL), f32),
        compiler_params=pltpu.CompilerParams(
            dimension_semantics=("arbitrary", "arbitrary"), vmem_limit_bytes=VMEM_LIMIT),
        name="ada",
    )(c, w_ada, b_ada.reshape(DEPTH, 1, 6 * D_MODEL))


def _proj_kernel(x_ref, sc_ref, sh_ref, w_ref, qs_ref, o_ref):
    h = (x_ref[0] * (1.0 + sc_ref[0]) + sh_ref[0]).astype(bf16)
    o_ref[0] = (_dot(h, w_ref[...]) * qs_ref[...]).astype(bf16)


def _proj_gate_kernel(x_ref, sc_ref, sh_ref, w_ref, qs_ref, wf_ref, o_ref, f_ref):
    h = (x_ref[0] * (1.0 + sc_ref[0]) + sh_ref[0]).astype(bf16)
    o_ref[0] = (_dot(h, w_ref[...]) * qs_ref[...]).astype(bf16)
    f_ref[0] = _dot(h, wf_ref[...])


def _resident(shape):
    zeros = (0,) * len(shape)
    return pl.BlockSpec(shape, lambda *_: zeros, pipeline_mode=pl.Buffered(1))


def _row_spec(width, rows=ROW_TILE):
    return pl.BlockSpec((1, rows, width), lambda b, i: (b, i, 0))


def _batch_vec_spec():
    return pl.BlockSpec((1, 1, D_MODEL), lambda b, i: (b, 0, 0))


def _in_proj(x, sc, sh, w, w_gate=None):
    B = x.shape[0]
    in_specs = [_row_spec(D_MODEL), _batch_vec_spec(), _batch_vec_spec(), _resident(w.shape),
                _resident((1, IN_MAIN))]
    out_specs = [_row_spec(IN_MAIN)]
    out_shape = [jax.ShapeDtypeStruct((B, SEQ, IN_MAIN), bf16)]
    args = [x, sc, sh, w, _q_column_scale()]
    kern = _proj_kernel
    if w_gate is not None:
        in_specs.append(_resident(w_gate.shape))
        out_specs.append(_row_spec(LANES))
        out_shape.append(jax.ShapeDtypeStruct((B, SEQ, LANES), f32))
        args.append(w_gate)
        kern = _proj_gate_kernel
    res = pl.pallas_call(
        kern,
        grid=(B, SEQ // ROW_TILE),
        in_specs=in_specs,
        out_specs=out_specs,
        out_shape=out_shape,
        compiler_params=pltpu.CompilerParams(
            dimension_semantics=("arbitrary", "arbitrary"), vmem_limit_bytes=VMEM_LIMIT),
        name="in_proj",
    )(*args)
    return res if w_gate is not None else res[0]


def _near_bias_t(tbl_ref, head):
    kk = lax.broadcasted_iota(jnp.int32, (2 * ATT_TILE, ATT_TILE), 0)
    qq = lax.broadcasted_iota(jnp.int32, (2 * ATT_TILE, ATT_TILE), 1)
    return (_t5_bias(qq - kk + ATT_TILE, tbl_ref, head) - tbl_ref[NUM_BUCKETS - 1, head]) * LOG2_E


def _add_near_bias(t, bias_ref, lead, queries, keys):
    rel = keys.start - (queries.start - ATT_TILE)
    if rel < 0:
        return t
    return t + bias_ref[lead + (slice(rel, rel + KEY_CHUNK), slice(None))]


def _smem_spec():
    return pl.BlockSpec(memory_space=pltpu.SMEM)


def _head_spec(first_block):
    return pl.BlockSpec((1, SEQ, LANES), lambda g, b: (b, 0, first_block + g))


def _diff_kernel(tbl_ref, lam_ref, g_ref, q_ref, k_ref, v_ref, o_ref, bias_sc, vt_sc, *,
                 lambda_init):
    h = pl.program_id(0)

    @pl.when(pl.program_id(1) == 0)
    def _():
        bias_sc[...] = _near_bias_t(tbl_ref, h)

    _store_transposed(v_ref, vt_sc, DV_A)
    lm = lam_ref[...]
    lam_full = (jnp.exp(jnp.sum(lm[0:1] * lm[1:2], axis=1, keepdims=True))
                - jnp.exp(jnp.sum(lm[2:3] * lm[3:4], axis=1, keepdims=True)) + lambda_init)

    def load_q(m, queries):
        return q_ref[0, queries, m * HEAD_DIM:(m + 1) * HEAD_DIM]

    def load_k(m, keys):
        return k_ref[0, keys, m * HEAD_DIM:(m + 1) * HEAD_DIM]

    def load_vt(m, keys):
        return vt_sc[:, keys]

    def score_mod(m, queries, keys, t):
        return _add_near_bias(t, bias_sc, (), queries, keys)

    def emit(queries, results):
        (acc1, l1), (acc2, l2) = results
        o = (acc1 / l1 - lam_full * (acc2 / l2)).T
        o = o * lax.rsqrt(jnp.mean(o * o, axis=1, keepdims=True) + LN_EPS)
        o_ref[0, queries, :] = (o * g_ref[...] * (1.0 - lambda_init)).astype(bf16)

    _causal_flash_t(range(2), load_q, load_k, load_vt, score_mod, lambda m, queries: None, emit)


def _diff_attention(qkv, table, lam, subln_g, lambda_init):
    B = qkv.shape[0]
    nblk = HALF // LANES
    return pl.pallas_call(
        functools.partial(_diff_kernel, lambda_init=lambda_init),
        grid=(N_HEADS_A, B),
        in_specs=[
            _smem_spec(),
            pl.BlockSpec((4, HEAD_DIM), lambda g, b: (0, 0)),
            pl.BlockSpec((1, DV_A), lambda g, b: (0, 0)),
            _head_spec(0), _head_spec(nblk), _head_spec(2 * nblk),
        ],
        out_specs=_head_spec(0),
        out_shape=jax.ShapeDtypeStruct((B, SEQ, HALF), bf16),
        scratch_shapes=[pltpu.VMEM((2 * ATT_TILE, ATT_TILE), f32),
                        pltpu.VMEM((DV_A + SUM_ROWS, SEQ), bf16)],
        compiler_params=pltpu.CompilerParams(
            dimension_semantics=("arbitrary", "arbitrary"), vmem_limit_bytes=VMEM_LIMIT),
        name="diff_attention",
    )(table, lam, subln_g.reshape(1, DV_A), qkv, qkv, qkv)


def _dil_kernel(tbl_ref, q_ref, k_ref, v_ref, o_ref, qf, kf, vf, bias_sc, osc, msc, dsc):
    hp = pl.program_id(0)
    W = DIL_BLOCK
    npat = len(DILATED_PATTERNS)
    row = lax.broadcasted_iota(jnp.int32, (W, 2 * W), 0)
    col = lax.broadcasted_iota(jnp.int32, (W, 2 * W), 1)
    du = row - col + W

    @pl.when(pl.program_id(1) == 0)
    def _():
        for p, (_, dil) in enumerate(DILATED_PATTERNS):
            for hh in range(2):
                bias = _t5_bias(du * dil, tbl_ref, N_HEADS_A + 2 * hp + hh)
                bias_sc[p, hh] = jnp.where((du >= 0) & (du <= W), bias * LOG2_E, NEG_INF)

    qf[...] = q_ref[0].astype(f32)
    kf[...] = k_ref[0].astype(f32)
    vf[...] = v_ref[0].astype(f32)
    lane = lax.broadcasted_iota(jnp.int32, (W, LANES), 1)
    head_lanes = [lane < HEAD_DIM, lane >= HEAD_DIM]
    blocks, staged_out = {}, {}

    def block(p, r, n):
        if (p, r, n) not in blocks:
            dil = DILATED_PATTERNS[p][1]
            rows = pl.ds(r + n * W * dil, W, stride=dil)
            blocks[(p, r, n)] = tuple(ref[rows, :].astype(bf16) for ref in (qf, kf, vf))
        return blocks[(p, r, n)]

    def scores(item):
        p, r, n, hh = item
        q, k, _ = block(p, r, n)
        q = jnp.where(head_lanes[hh], q, jnp.zeros_like(q))
        if n > 0:
            k = jnp.concatenate([block(p, r, n - 1)[1], k], axis=0)
            s = _dot_nt(q, k) + bias_sc[p, hh]
        else:
            s = _dot_nt(q, k) + bias_sc[p, hh, :, W:]
        return s, jnp.max(s, axis=1, keepdims=True)

    def finish(item, staged):
        p, r, n, hh = item
        s, m = staged
        v = block(p, r, n)[2]
        if n > 0:
            v = jnp.concatenate([block(p, r, n - 1)[2], v], axis=0)
        e = jnp.exp2(s - m)
        den = jnp.sum(e, axis=1, keepdims=True)
        num = _dot(e.astype(bf16), v)
        staged_out[hh] = (num, m, den)
        if hh == 1:
            dil = DILATED_PATTERNS[p][1]
            rows = pl.ds(r + n * W * dil, W, stride=dil)
            for ref, a0, a1 in zip((osc, msc, dsc), staged_out[0], staged_out[1]):
                ref[p, rows, :] = jnp.where(head_lanes[0], a0, a1)

    items = []
    for p, (window, dil) in enumerate(DILATED_PATTERNS):
        assert window // dil == W
        nb = SEQ // dil // W
        items += [(p, r, n, hh) for r in range(dil) for n in range(nb) for hh in range(2)]
    _software_pipeline(items, scores, finish)

    def merge_body(i, _):
        rows = pl.ds(pl.multiple_of(i * ATT_TILE, ATT_TILE), ATT_TILE)
        ms = [msc[p, rows, :] for p in range(npat)]
        mx = functools.reduce(jnp.maximum, ms)
        ws = [jnp.exp2(m - mx) for m in ms]
        add = lambda a, b: a + b
        numer = functools.reduce(add, [w * osc[p, rows, :] for p, w in enumerate(ws)])
        denom = functools.reduce(add, [w * dsc[p, rows, :] for p, w in enumerate(ws)])
        o_ref[0, rows, :] = (numer / denom).astype(bf16)
        return 0

    lax.fori_loop(0, N_ATT_TILES, merge_body, 0)


def _dilated_attention(qkv, table):
    B = qkv.shape[0]
    nblk = HALF // LANES
    first = 3 * nblk
    npat = len(DILATED_PATTERNS)
    return pl.pallas_call(
        _dil_kernel,
        grid=(N_HEADS_B // 2, B),
        in_specs=[_smem_spec(), _head_spec(first), _head_spec(first + nblk),
                  _head_spec(first + 2 * nblk)],
        out_specs=_head_spec(0),
        out_shape=jax.ShapeDtypeStruct((B, SEQ, HALF), bf16),
        scratch_shapes=[
            pltpu.VMEM((SEQ, LANES), f32), pltpu.VMEM((SEQ, LANES), f32), pltpu.VMEM((SEQ, LANES), f32),
            pltpu.VMEM((npat, 2, DIL_BLOCK, 2 * DIL_BLOCK), f32),
            pltpu.VMEM((npat, SEQ, LANES), f32), pltpu.VMEM((npat, SEQ, LANES), f32),
            pltpu.VMEM((npat, SEQ, LANES), f32),
        ],
        compiler_params=pltpu.CompilerParams(
            dimension_semantics=("arbitrary", "arbitrary"), vmem_limit_bytes=VMEM_LIMIT),
        name="dilated_attention",
    )(table, qkv, qkv, qkv)


def _moba_kernel(tbl_ref, q_ref, k_ref, v_ref, o_ref, bias_sc, vt_sc):
    hp = pl.program_id(0)
    assert MOBA_BLOCK == ATT_TILE == KEY_CHUNK

    @pl.when(pl.program_id(1) == 0)
    def _():
        for hh in range(2):
            bias_sc[hh] = _near_bias_t(tbl_ref, 2 * hp + hh)

    _store_transposed(v_ref, vt_sc, HEAD_DIM)

    def load_q(hh, queries):
        return q_ref[0, queries, hh * HEAD_DIM:(hh + 1) * HEAD_DIM]

    def load_k(hh, keys):
        return k_ref[0, keys, hh * HEAD_DIM:(hh + 1) * HEAD_DIM]

    def load_vt(hh, keys):
        return vt_sc[_vt_rows(hh, HEAD_DIM), keys]

    k_means = []
    for hh in range(2):
        means = [jnp.sum(load_k(hh, slice(n * MOBA_BLOCK, (n + 1) * MOBA_BLOCK)).astype(f32), axis=0,
                         keepdims=True) * (1.0 / MOBA_BLOCK) for n in range(N_MOBA_BLOCKS)]
        k_means.append(jnp.concatenate(means + [jnp.zeros_like(m) for m in means], axis=0).astype(bf16))

    block_masks = {}

    def block_mask(hh, queries, blk):
        own = queries.start // MOBA_BLOCK
        if own <= MOBA_TOPK:
            return None
        if (hh, own) not in block_masks:
            gate = _dot_nt(k_means[hh], load_q(hh, queries))
            g = [gate[n:n + 1, :] for n in range(own)]
            masks = []
            for n in range(own):
                rank = jnp.zeros((1, ATT_TILE), jnp.int32)
                for m in range(own):
                    if m != n:
                        beats = (g[m] >= g[n]) if m < n else (g[m] > g[n])
                        rank = rank + beats.astype(jnp.int32)
                masks.append(jnp.where(rank < MOBA_TOPK, 0.0, NEG_INF))
            block_masks[(hh, own)] = masks
        return block_masks[(hh, own)][blk]

    def score_mod(hh, queries, keys, t):
        t = _add_near_bias(t, bias_sc, (hh,), queries, keys)
        if keys.start < queries.start:
            mask = block_mask(hh, queries, keys.start // MOBA_BLOCK)
            if mask is not None:
                t = t + mask
        return t

    def emit(queries, results):
        o_t = jnp.concatenate([acc / l for acc, l in results], axis=0)
        o_ref[0, queries, :] = o_t.T.astype(bf16)

    _causal_flash_t(range(2), load_q, load_k, load_vt, score_mod, lambda hh, queries: None, emit)


def _moba_attention(qkv, table):
    B = qkv.shape[0]
    nblk = HALF // LANES
    return pl.pallas_call(
        _moba_kernel,
        grid=(N_HEADS_C // 2, B),
        in_specs=[_smem_spec(), _head_spec(0), _head_spec(nblk), _head_spec(2 * nblk)],
        out_specs=_head_spec(0),
        out_shape=jax.ShapeDtypeStruct((B, SEQ, HALF), bf16),
        scratch_shapes=[pltpu.VMEM((2, 2 * ATT_TILE, ATT_TILE), f32),
                        pltpu.VMEM((2 * (HEAD_DIM + SUM_ROWS), SEQ), bf16)],
        compiler_params=pltpu.CompilerParams(
            dimension_semantics=("arbitrary", "arbitrary"), vmem_limit_bytes=VMEM_LIMIT),
        name="moba_attention",
    )(table, qkv, qkv, qkv)


def _forget_cum_kernel(fd_ref, fb_ref, col_ref, row_ref):
    x = jax.nn.log_sigmoid(fd_ref[0] + fb_ref[...])
    t = lax.broadcasted_iota(jnp.int32, (SEQ, LANES), 0)
    shift = 1
    while shift < SEQ:
        x = x + jnp.where(t >= shift, pltpu.roll(x, shift, axis=0), 0.0)
        shift *= 2
    col_ref[0] = x
    for c in range(SEQ // LANES):
        blk = x[c * LANES:(c + 1) * LANES, :].T
        row_ref[0, :, c * LANES:(c + 1) * LANES] = blk[0:N_HEADS_D, :]


def _forget_cum(fd, forget_b):
    B = fd.shape[0]
    fb = jnp.zeros((1, LANES), f32).at[0, :N_HEADS_D].set(forget_b)
    return pl.pallas_call(
        _forget_cum_kernel,
        grid=(B,),
        in_specs=[pl.BlockSpec((1, SEQ, LANES), lambda b: (b, 0, 0)),
                  pl.BlockSpec((1, LANES), lambda b: (0, 0))],
        out_specs=[pl.BlockSpec((1, SEQ, LANES), lambda b: (b, 0, 0)),
                   pl.BlockSpec((1, N_HEADS_D, SEQ), lambda b: (b, 0, 0))],
        out_shape=[jax.ShapeDtypeStruct((B, SEQ, LANES), f32),
                   jax.ShapeDtypeStruct((B, N_HEADS_D, SEQ), f32)],
        compiler_params=pltpu.CompilerParams(
            dimension_semantics=("arbitrary",), vmem_limit_bytes=VMEM_LIMIT),
        name="forget_cum",
    )(fd, fb)


def _fox_kernel(ccol_ref, crow_ref, q_ref, k_ref, v_ref, o_ref, col_sc, vt_sc):
    hp = pl.program_id(0)
    lane = lax.broadcasted_iota(jnp.int32, (SEQ, LANES), 1)
    for hh in range(2):
        col_sc[hh] = jnp.sum(jnp.where(lane == 2 * hp + hh, ccol_ref[0], 0.0), axis=1,
                             keepdims=True) * LOG2_E
    _store_transposed(v_ref, vt_sc, HEAD_DIM)

    def load_q(hh, queries):
        return q_ref[0, queries, hh * HEAD_DIM:(hh + 1) * HEAD_DIM]

    def load_k(hh, keys):
        return k_ref[0, keys, hh * HEAD_DIM:(hh + 1) * HEAD_DIM]

    def load_vt(hh, keys):
        return vt_sc[_vt_rows(hh, HEAD_DIM), keys]

    def score_mod(hh, queries, keys, t):
        return t - col_sc[hh, keys, :]

    def query_shift(hh, queries):
        return crow_ref[0, pl.ds(2 * hp + hh, 1), queries] * LOG2_E

    def emit(queries, results):
        o_t = jnp.concatenate([acc / l for acc, l in results], axis=0)
        o_ref[0, queries, :] = o_t.T.astype(bf16)

    _causal_flash_t(range(2), load_q, load_k, load_vt, score_mod, query_shift, emit)


def _fox_attention(qkv, cum_col, cum_row):
    B = qkv.shape[0]
    nblk = HALF // LANES
    first = 3 * nblk
    return pl.pallas_call(
        _fox_kernel,
        grid=(N_HEADS_D // 2, B),
        in_specs=[pl.BlockSpec((1, SEQ, LANES), lambda g, b: (b, 0, 0)),
                  pl.BlockSpec((1, N_HEADS_D, SEQ), lambda g, b: (b, 0, 0)),
                  _head_spec(first), _head_spec(first + nblk), _head_spec(first + 2 * nblk)],
        out_specs=_head_spec(0),
        out_shape=jax.ShapeDtypeStruct((B, SEQ, HALF), bf16),
        scratch_shapes=[pltpu.VMEM((2, SEQ, 1), f32), pltpu.VMEM((2 * (HEAD_DIM + SUM_ROWS), SEQ), bf16)],
        compiler_params=pltpu.CompilerParams(
            dimension_semantics=("arbitrary", "arbitrary"), vmem_limit_bytes=VMEM_LIMIT),
        name="fox_attention",
    )(cum_col, cum_row, qkv, qkv, qkv)


def _post_kernel(o1_ref, o2_ref, x_ref, g1_ref, sc2_ref, sh2_ref, g2_ref, ln_ref,
                 wo_ref, wi_ref, wout_ref, out_ref):
    ln = ln_ref[...]
    n_sub = POST_ROW_TILE // ROW_SUB

    def mix(r):
        rows = slice(r * ROW_SUB, (r + 1) * ROW_SUB)
        y = _dot(o1_ref[0, rows, :], wo_ref[0:HALF, :]) + _dot(o2_ref[0, rows, :], wo_ref[HALF:, :])
        x1 = _layer_norm(DEEPNORM_ALPHA * x_ref[0, rows, :] + (1.0 + g1_ref[0]) * y, ln[0:1], ln[1:2])
        return x1, (x1 * (1.0 + sc2_ref[0]) + sh2_ref[0]).astype(bf16)

    def ffn_chunk(h, chunk):
        c0, c1 = chunk
        g = _dot(h, wi_ref[:, c0:c1])
        u = _dot(h, wi_ref[:, D_FF + c0:D_FF + c1])
        return _dot((g * jax.nn.sigmoid(g) * u).astype(bf16), wout_ref[c0:c1, :])

    def finish(r, x1, y2):
        rows = slice(r * ROW_SUB, (r + 1) * ROW_SUB)
        out_ref[0, rows, :] = _layer_norm(DEEPNORM_ALPHA * x1 + (1.0 + g2_ref[0]) * y2,
                                          ln[2:3], ln[3:4])

    staged, done = mix(0), None
    for r in range(n_sub):
        x1, h = staged
        y2 = ffn_chunk(h, FF_CHUNKS[0])
        if done is not None:
            finish(*done)
        if r + 1 < n_sub:
            staged = mix(r + 1)
        done = (r, x1, y2 + ffn_chunk(h, FF_CHUNKS[1]))
    finish(*done)


def _post(o1, o2, x, g1, sc2, sh2, g2, ln, w_o, w_ffn_in, w_ffn_out):
    B = x.shape[0]
    return pl.pallas_call(
        _post_kernel,
        grid=(B, SEQ // POST_ROW_TILE),
        in_specs=[_row_spec(HALF, POST_ROW_TILE), _row_spec(HALF, POST_ROW_TILE),
                  _row_spec(D_MODEL, POST_ROW_TILE),
                  _batch_vec_spec(), _batch_vec_spec(), _batch_vec_spec(), _batch_vec_spec(),
                  pl.BlockSpec((4, D_MODEL), lambda b, i: (0, 0)),
                  _resident(w_o.shape), _resident(w_ffn_in.shape), _resident(w_ffn_out.shape)],
        out_specs=_row_spec(D_MODEL, POST_ROW_TILE),
        out_shape=jax.ShapeDtypeStruct((B, SEQ, D_MODEL), f32),
        compiler_params=pltpu.CompilerParams(
            dimension_semantics=("arbitrary", "arbitrary"), vmem_limit_bytes=VMEM_LIMIT),
        name="out_proj_ffn",
    )(o1, o2, x, g1, sc2, sh2, g2, ln, w_o, w_ffn_in, w_ffn_out)


def _q_column_scale():
    scale = np.ones((1, IN_MAIN), np.float32)
    scale[0, 0:HALF] = ATTN_SCALE * LOG2_E
    scale[0, 3 * HALF:4 * HALF] = ATTN_SCALE * LOG2_E
    return jnp.asarray(scale)


def kernel(x, c, rel_bias, w_ada, b_ada, ln_g, ln_b, w_in_ab, diff_lambda, diff_subln_g,
           w_in_cd, forget_b, w_o, w_ffn_in, w_ffn_out):
    B = x.shape[0]
    ada = _ada(c, w_ada, b_ada)
    for l in range(DEPTH):
        sh1, sc1, g1, sh2, sc2, g2 = [a.reshape(B, 1, D_MODEL) for a in jnp.split(ada[l], 6, axis=-1)]
        i = l // 2
        if l % 2 == 0:
            lambda_init = 0.8 - 0.6 * math.exp(-0.3 * l)
            qkv = _in_proj(x, sc1, sh1, w_in_ab[i].astype(bf16))
            o1 = _diff_attention(qkv, rel_bias, diff_lambda[i], diff_subln_g[i], lambda_init)
            o2 = _dilated_attention(qkv, rel_bias)
        else:
            w_gate = jnp.zeros((D_MODEL, LANES), f32).at[:, :N_HEADS_D].set(w_in_cd[i][:, IN_MAIN:])
            qkv, fd = _in_proj(x, sc1, sh1, w_in_cd[i][:, :IN_MAIN].astype(bf16), w_gate.astype(bf16))
            cum_col, cum_row = _forget_cum(fd, forget_b[i])
            o1 = _moba_attention(qkv, rel_bias)
            o2 = _fox_attention(qkv, cum_col, cum_row)
        ln = jnp.stack([ln_g[l, 0], ln_b[l, 0], ln_g[l, 1], ln_b[l, 1]])
        x = _post(o1, o2, x, g1, sc2, sh2, g2, ln, w_o[l].astype(bf16),
                  w_ffn_in[l].astype(bf16), w_ffn_out[l].astype(bf16))
    return x
```

```python
import functools
import math

import numpy as np
import jax
import jax.numpy as jnp
from jax import lax
from jax.experimental import pallas as pl
from jax.experimental.pallas import tpu as pltpu

f32 = jnp.float32
bf16 = jnp.bfloat16

D_MODEL = 1024
SEQ = 2048
DEPTH = 2
HEAD_DIM = 64
N_HEADS_A = 4
DV_A = 2 * HEAD_DIM
N_HEADS_B = 8
DILATED_PATTERNS = ((128, 1), (512, 4), (2048, 16))
DIL_BLOCK = 128
N_HEADS_C = 8
MOBA_BLOCK = 256
MOBA_TOPK = 3
N_MOBA_BLOCKS = SEQ // MOBA_BLOCK
N_HEADS_D = 8
NUM_BUCKETS = 32
MAX_DISTANCE = 128
D_FF = 2816
LN_EPS = 1e-5
DEEPNORM_ALPHA = (2 * DEPTH) ** 0.25
ATTN_SCALE = HEAD_DIM ** -0.5
LOG2_E = math.log2(math.e)
HALF = D_MODEL // 2
IN_MAIN = 3 * D_MODEL

LANES = 128
ATT_TILE = 256
KEY_CHUNK = 256
N_ATT_TILES = SEQ // ATT_TILE
SUM_ROWS = 16
PIPELINE_DEPTH = 6
MXU_DIM = 256
FF_CHUNKS = ((0, 6 * MXU_DIM), (6 * MXU_DIM, D_FF))
ROW_TILE = 1024
POST_ROW_TILE = 1024
ROW_SUB = 256
VMEM_LIMIT = 56 * 1024 * 1024

NEG_INF = float("-inf")


def _t5_bucket_starts():
    n = np.arange(0, 4 * SEQ, dtype=np.int64)
    max_exact = NUM_BUCKETS // 2
    nf = np.maximum(n, 1).astype(np.float32)
    val = (np.log(nf / np.float32(max_exact)) / np.float32(math.log(MAX_DISTANCE / max_exact))
           * np.float32(NUM_BUCKETS - max_exact))
    large = np.minimum(max_exact + val.astype(np.int32), NUM_BUCKETS - 1)
    bucket = np.where(n < max_exact, n, large)
    assert np.all(np.diff(bucket) >= 0)
    return tuple(int(np.argmax(bucket >= b)) for b in range(NUM_BUCKETS))


T5_STARTS = _t5_bucket_starts()
assert T5_STARTS[-1] <= ATT_TILE + 1


def _t5_bias(dist, tbl_ref, col):
    out = jnp.full(dist.shape, tbl_ref[0, col], f32)
    for b in range(1, NUM_BUCKETS):
        out = jnp.where(dist >= T5_STARTS[b], tbl_ref[b, col], out)
    return out


def _dot_nt(a, b):
    return lax.dot_general(a, b, (((1,), (1,)), ((), ())), preferred_element_type=f32)


def _dot(a, b):
    return jnp.dot(a, b, preferred_element_type=f32)


def _software_pipeline(items, first_stage, second_stage):
    pending = [first_stage(it) for it in items[:PIPELINE_DEPTH]]
    for idx, it in enumerate(items):
        cur = pending.pop(0)
        if idx + PIPELINE_DEPTH < len(items):
            pending.append(first_stage(items[idx + PIPELINE_DEPTH]))
        second_stage(it, cur)


def _store_transposed(v_ref, vt_ref, group):
    ones = jnp.ones((SUM_ROWS, ATT_TILE), bf16)
    for c in range(N_ATT_TILES):
        rows = slice(c * ATT_TILE, (c + 1) * ATT_TILE)
        vt = v_ref[0, rows, :].astype(f32).T.astype(bf16)
        for g in range(LANES // group):
            base = g * (group + SUM_ROWS)
            vt_ref[base:base + group, rows] = vt[g * group:(g + 1) * group, :]
            vt_ref[base + group:base + group + SUM_ROWS, rows] = ones


def _vt_rows(g, group):
    return slice(g * (group + SUM_ROWS), (g + 1) * (group + SUM_ROWS))


def _causal_flash_t(streams, load_q, load_k, load_vt, score_mod, query_shift, emit):
    streams = list(streams)
    krow = lax.broadcasted_iota(jnp.int32, (KEY_CHUNK, ATT_TILE), 0)
    qcol = lax.broadcasted_iota(jnp.int32, (KEY_CHUNK, ATT_TILE), 1)
    per_tile = ATT_TILE // KEY_CHUNK
    state = {}

    def spans(work):
        _, i, c = work
        return (slice(i * ATT_TILE, (i + 1) * ATT_TILE), slice(c * KEY_CHUNK, (c + 1) * KEY_CHUNK))

    def first_stage(work):
        s, i, c = work
        queries, keys = spans(work)
        t = score_mod(s, queries, keys, _dot_nt(load_k(s, keys), load_q(s, queries)))
        if keys.stop - 1 > queries.start:
            t = jnp.where(krow + (keys.start - queries.start) <= qcol, t, NEG_INF)
        return t, jnp.max(t, axis=0, keepdims=True)

    def second_stage(work, staged):
        s, i, c = work
        queries, keys = spans(work)
        t, tmax = staged
        m, l, acc = state.get((s, i), (None, None, None))
        m_new = tmax if m is None else jnp.maximum(m, tmax)
        shift = query_shift(s, queries)
        if shift is None:
            p = jnp.exp2(t - m_new)
        else:
            p = jnp.exp2(t + (shift - (m_new + shift)))
        pv = _dot(load_vt(s, keys), p.astype(bf16))
        dv = pv.shape[0] - SUM_ROWS
        pv, psum = pv[:dv], pv[dv:dv + 1]
        if m is None:
            l, acc = psum, pv
        else:
            alpha = jnp.exp2(m - m_new)
            l, acc = alpha * l + psum, alpha * acc + pv
        state[(s, i)] = (m_new, l, acc)
        if (s, c) == last_of_tile[i]:
            emit(queries, [(state[(s2, i)][2], state[(s2, i)][1]) for s2 in streams])

    work, last_of_tile = [], {}
    for i in reversed(range(N_ATT_TILES)):
        first = i * per_tile
        order = [first] + [c for c in range((i + 1) * per_tile) if c != first]
        work += [(s, i, c) for c in order for s in streams]
        last_of_tile[i] = (streams[-1], order[-1])
    _software_pipeline(work, first_stage, second_stage)


def _tile_iotas(n):
    row = lax.broadcasted_iota(jnp.int32, (n, n), 0)
    col = lax.broadcasted_iota(jnp.int32, (n, n), 1)
    return row, col


def _layer_norm(x, g, b):
    mu = jnp.mean(x, axis=1, keepdims=True)
    xc = x - mu
    var = jnp.mean(xc * xc, axis=1, keepdims=True)
    return xc * lax.rsqrt(var + LN_EPS) * g + b


def _ada_kernel(c_ref, w_ref, b_ref, o_ref):
    c = c_ref[...]
    cs = c * jax.nn.sigmoid(c)
    o_ref[0] = _dot(cs.astype(bf16), w_ref[0].astype(bf16)) + b_ref[0]


def _ada(c, w_ada, b_ada):
    B = c.shape[0]
    tn = 1536
    return pl.pallas_call(
        _ada_kernel,
        grid=(DEPTH, 6 * D_MODEL // tn),
        in_specs=[
            pl.BlockSpec((B, D_MODEL), lambda l, j: (0, 0)),
            pl.BlockSpec((1, D_MODEL, tn), lambda l, j: (l, 0, j)),
            pl.BlockSpec((1, 1, tn), lambda l, j: (l, 0, j)),
        ],
        out_specs=pl.BlockSpec((1, B, tn), lambda l, j: (l, 0, j)),
        out_shape=jax.ShapeDtypeStruct((DEPTH, B, 6 * D_MODEL), f32),
        compiler_params=pltpu.CompilerParams(
            dimension_semantics=("arbitrary", "arbitrary"), vmem_limit_bytes=VMEM_LIMIT),
        name="ada",
    )(c, w_ada, b_ada.reshape(DEPTH, 1, 6 * D_MODEL))


def _proj_kernel(x_ref, sc_ref, sh_ref, w_ref, qs_ref, o_ref):
    h = (x_ref[0] * (1.0 + sc_ref[0]) + sh_ref[0]).astype(bf16)
    o_ref[0] = (_dot(h, w_ref[...]) * qs_ref[...]).astype(bf16)


def _proj_gate_kernel(x_ref, sc_ref, sh_ref, w_ref, qs_ref, wf_ref, o_ref, f_ref):
    h = (x_ref[0] * (1.0 + sc_ref[0]) + sh_ref[0]).astype(bf16)
    o_ref[0] = (_dot(h, w_ref[...]) * qs_ref[...]).astype(bf16)
    f_ref[0] = _dot(h, wf_ref[...])


def _resident(shape):
    zeros = (0,) * len(shape)
    return pl.BlockSpec(shape, lambda *_: zeros, pipeline_mode=pl.Buffered(1))


def _row_spec(width, rows=ROW_TILE):
    return pl.BlockSpec((1, rows, width), lambda b, i: (b, i, 0))


def _batch_vec_spec():
    return pl.BlockSpec((1, 1, D_MODEL), lambda b, i: (b, 0, 0))


def _in_proj(x, sc, sh, w, w_gate=None):
    B = x.shape[0]
    in_specs = [_row_spec(D_MODEL), _batch_vec_spec(), _batch_vec_spec(), _resident(w.shape),
                _resident((1, IN_MAIN))]
    out_specs = [_row_spec(IN_MAIN)]
    out_shape = [jax.ShapeDtypeStruct((B, SEQ, IN_MAIN), bf16)]
    args = [x, sc, sh, w, _q_column_scale()]
    kern = _proj_kernel
    if w_gate is not None:
        in_specs.append(_resident(w_gate.shape))
        out_specs.append(_row_spec(LANES))
        out_shape.append(jax.ShapeDtypeStruct((B, SEQ, LANES), f32))
        args.append(w_gate)
        kern = _proj_gate_kernel
    res = pl.pallas_call(
        kern,
        grid=(B, SEQ // ROW_TILE),
        in_specs=in_specs,
        out_specs=out_specs,
        out_shape=out_shape,
        compiler_params=pltpu.CompilerParams(
            dimension_semantics=("arbitrary", "arbitrary"), vmem_limit_bytes=VMEM_LIMIT),
        name="in_proj",
    )(*args)
    return res if w_gate is not None else res[0]


def _near_bias_t(tbl_ref, head):
    kk = lax.broadcasted_iota(jnp.int32, (2 * ATT_TILE, ATT_TILE), 0)
    qq = lax.broadcasted_iota(jnp.int32, (2 * ATT_TILE, ATT_TILE), 1)
    return (_t5_bias(qq - kk + ATT_TILE, tbl_ref, head) - tbl_ref[NUM_BUCKETS - 1, head]) * LOG2_E


def _add_near_bias(t, bias_ref, lead, queries, keys):
    rel = keys.start - (queries.start - ATT_TILE)
    if rel < 0:
        return t
    return t + bias_ref[lead + (slice(rel, rel + KEY_CHUNK), slice(None))]


def _attention_params():
    return pltpu.CompilerParams(
        dimension_semantics=("arbitrary", "arbitrary"), vmem_limit_bytes=VMEM_LIMIT)


def _smem_spec():
    return pl.BlockSpec(memory_space=pltpu.SMEM)


def _head_spec(first_block):
    return pl.BlockSpec((1, SEQ, LANES), lambda g, b: (b, 0, first_block + g))


def _diff_kernel(tbl_ref, lam_ref, g_ref, q_ref, k_ref, v_ref, o_ref, bias_sc, vt_sc, *,
                 lambda_init):
    h = pl.program_id(0)

    @pl.when(pl.program_id(1) == 0)
    def _():
        bias_sc[...] = _near_bias_t(tbl_ref, h)

    _store_transposed(v_ref, vt_sc, DV_A)
    lm = lam_ref[...]
    lam_full = (jnp.exp(jnp.sum(lm[0:1] * lm[1:2], axis=1, keepdims=True))
                - jnp.exp(jnp.sum(lm[2:3] * lm[3:4], axis=1, keepdims=True)) + lambda_init)

    def load_q(m, queries):
        return q_ref[0, queries, m * HEAD_DIM:(m + 1) * HEAD_DIM]

    def load_k(m, keys):
        return k_ref[0, keys, m * HEAD_DIM:(m + 1) * HEAD_DIM]

    def load_vt(m, keys):
        return vt_sc[:, keys]

    def score_mod(m, queries, keys, t):
        return _add_near_bias(t, bias_sc, (), queries, keys)

    def emit(queries, results):
        (acc1, l1), (acc2, l2) = results
        o = (acc1 / l1 - lam_full * (acc2 / l2)).T
        o = o * lax.rsqrt(jnp.mean(o * o, axis=1, keepdims=True) + LN_EPS)
        o_ref[0, queries, :] = (o * g_ref[...] * (1.0 - lambda_init)).astype(bf16)

    _causal_flash_t(range(2), load_q, load_k, load_vt, score_mod, lambda m, queries: None, emit)


def _diff_attention(qkv, table, lam, subln_g, lambda_init):
    B = qkv.shape[0]
    nblk = HALF // LANES
    return pl.pallas_call(
        functools.partial(_diff_kernel, lambda_init=lambda_init),
        grid=(N_HEADS_A, B),
        in_specs=[
            _smem_spec(),
            pl.BlockSpec((4, HEAD_DIM), lambda g, b: (0, 0)),
            pl.BlockSpec((1, DV_A), lambda g, b: (0, 0)),
            _head_spec(0), _head_spec(nblk), _head_spec(2 * nblk),
        ],
        out_specs=_head_spec(0),
        out_shape=jax.ShapeDtypeStruct((B, SEQ, HALF), bf16),
        scratch_shapes=[pltpu.VMEM((2 * ATT_TILE, ATT_TILE), f32),
                        pltpu.VMEM((DV_A + SUM_ROWS, SEQ), bf16)],
        compiler_params=_attention_params(),
        name="diff_attention",
    )(table, lam, subln_g.reshape(1, DV_A), qkv, qkv, qkv)


def _dil_kernel(tbl_ref, q_ref, k_ref, v_ref, o_ref, qf, kf, vf, bias_sc, osc, msc, dsc):
    hp = pl.program_id(0)
    W = DIL_BLOCK
    npat = len(DILATED_PATTERNS)
    row = lax.broadcasted_iota(jnp.int32, (W, 2 * W), 0)
    col = lax.broadcasted_iota(jnp.int32, (W, 2 * W), 1)
    du = row - col + W

    @pl.when(pl.program_id(1) == 0)
    def _():
        for p, (_, dil) in enumerate(DILATED_PATTERNS):
            for hh in range(2):
                bias = _t5_bias(du * dil, tbl_ref, N_HEADS_A + 2 * hp + hh)
                bias_sc[p, hh] = jnp.where((du >= 0) & (du <= W), bias * LOG2_E, NEG_INF)

    qf[...] = q_ref[0].astype(f32)
    kf[...] = k_ref[0].astype(f32)
    vf[...] = v_ref[0].astype(f32)
    lane = lax.broadcasted_iota(jnp.int32, (W, LANES), 1)
    head_lanes = [lane < HEAD_DIM, lane >= HEAD_DIM]
    blocks, staged_out = {}, {}

    def block(p, r, n):
        if (p, r, n) not in blocks:
            dil = DILATED_PATTERNS[p][1]
            rows = pl.ds(r + n * W * dil, W, stride=dil)
            blocks[(p, r, n)] = tuple(ref[rows, :].astype(bf16) for ref in (qf, kf, vf))
        return blocks[(p, r, n)]

    def scores(item):
        p, r, n, hh = item
        q, k, _ = block(p, r, n)
        q = jnp.where(head_lanes[hh], q, jnp.zeros_like(q))
        if n > 0:
            k = jnp.concatenate([block(p, r, n - 1)[1], k], axis=0)
            s = _dot_nt(q, k) + bias_sc[p, hh]
        else:
            s = _dot_nt(q, k) + bias_sc[p, hh, :, W:]
        return s, jnp.max(s, axis=1, keepdims=True)

    def finish(item, staged):
        p, r, n, hh = item
        s, m = staged
        v = block(p, r, n)[2]
        if n > 0:
            v = jnp.concatenate([block(p, r, n - 1)[2], v], axis=0)
        e = jnp.exp2(s - m)
        num = _dot(e.astype(bf16), jnp.concatenate([v, jnp.ones_like(v)], axis=1))
        staged_out[hh] = (num[:, :LANES], m, num[:, LANES:])
        if hh == 1:
            dil = DILATED_PATTERNS[p][1]
            rows = pl.ds(r + n * W * dil, W, stride=dil)
            for ref, a0, a1 in zip((osc, msc, dsc), staged_out[0], staged_out[1]):
                ref[p, rows, :] = jnp.where(head_lanes[0], a0, a1)

    items = []
    for p, (window, dil) in enumerate(DILATED_PATTERNS):
        assert window // dil == W
        nb = SEQ // dil // W
        items += [(p, r, n, hh) for r in range(dil) for n in range(nb) for hh in range(2)]
    _software_pipeline(items, scores, finish)

    def merge_body(i, _):
        rows = pl.ds(pl.multiple_of(i * ATT_TILE, ATT_TILE), ATT_TILE)
        ms = [msc[p, rows, :] for p in range(npat)]
        mx = functools.reduce(jnp.maximum, ms)
        ws = [jnp.exp2(m - mx) for m in ms]
        add = lambda a, b: a + b
        numer = functools.reduce(add, [w * osc[p, rows, :] for p, w in enumerate(ws)])
        denom = functools.reduce(add, [w * dsc[p, rows, :] for p, w in enumerate(ws)])
        o_ref[0, rows, :] = (numer / denom).astype(bf16)
        return 0

    lax.fori_loop(0, N_ATT_TILES, merge_body, 0)


def _dilated_attention(qkv, table):
    B = qkv.shape[0]
    nblk = HALF // LANES
    first = 3 * nblk
    npat = len(DILATED_PATTERNS)
    return pl.pallas_call(
        _dil_kernel,
        grid=(N_HEADS_B // 2, B),
        in_specs=[_smem_spec(), _head_spec(first), _head_spec(first + nblk),
                  _head_spec(first + 2 * nblk)],
        out_specs=_head_spec(0),
        out_shape=jax.ShapeDtypeStruct((B, SEQ, HALF), bf16),
        scratch_shapes=[
            pltpu.VMEM((SEQ, LANES), f32), pltpu.VMEM((SEQ, LANES), f32), pltpu.VMEM((SEQ, LANES), f32),
            pltpu.VMEM((npat, 2, DIL_BLOCK, 2 * DIL_BLOCK), f32),
            pltpu.VMEM((npat, SEQ, LANES), f32), pltpu.VMEM((npat, SEQ, LANES), f32),
            pltpu.VMEM((npat, SEQ, LANES), f32),
        ],
        compiler_params=_attention_params(),
        name="dilated_attention",
    )(table, qkv, qkv, qkv)


def _moba_kernel(tbl_ref, q_ref, k_ref, v_ref, o_ref, bias_sc, vt_sc):
    hp = pl.program_id(0)
    assert MOBA_BLOCK == ATT_TILE and MOBA_BLOCK % KEY_CHUNK == 0

    @pl.when(pl.program_id(1) == 0)
    def _():
        for hh in range(2):
            bias_sc[hh] = _near_bias_t(tbl_ref, 2 * hp + hh)

    _store_transposed(v_ref, vt_sc, HEAD_DIM)

    def load_q(hh, queries):
        return q_ref[0, queries, hh * HEAD_DIM:(hh + 1) * HEAD_DIM]

    def load_k(hh, keys):
        return k_ref[0, keys, hh * HEAD_DIM:(hh + 1) * HEAD_DIM]

    def load_vt(hh, keys):
        return vt_sc[_vt_rows(hh, HEAD_DIM), keys]

    k_means = []
    for hh in range(2):
        means = [jnp.sum(load_k(hh, slice(n * MOBA_BLOCK, (n + 1) * MOBA_BLOCK)).astype(f32), axis=0,
                         keepdims=True) * (1.0 / MOBA_BLOCK) for n in range(N_MOBA_BLOCKS)]
        k_means.append(jnp.concatenate(means + [jnp.zeros_like(m) for m in means], axis=0).astype(bf16))

    block_masks = {}

    def block_mask(hh, queries, blk):
        own = queries.start // MOBA_BLOCK
        if own <= MOBA_TOPK:
            return None
        if (hh, own) not in block_masks:
            gate = _dot_nt(k_means[hh], load_q(hh, queries))
            g = [gate[n:n + 1, :] for n in range(own)]
            masks = []
            for n in range(own):
                rank = jnp.zeros((1, ATT_TILE), jnp.int32)
                for m in range(own):
                    if m != n:
                        beats = (g[m] >= g[n]) if m < n else (g[m] > g[n])
                        rank = rank + beats.astype(jnp.int32)
                masks.append(jnp.where(rank < MOBA_TOPK, 0.0, NEG_INF))
            block_masks[(hh, own)] = masks
        return block_masks[(hh, own)][blk]

    def score_mod(hh, queries, keys, t):
        t = _add_near_bias(t, bias_sc, (hh,), queries, keys)
        if keys.start < queries.start:
            mask = block_mask(hh, queries, keys.start // MOBA_BLOCK)
            if mask is not None:
                t = t + mask
        return t

    def emit(queries, results):
        o_t = jnp.concatenate([acc / l for acc, l in results], axis=0)
        o_ref[0, queries, :] = o_t.T.astype(bf16)

    _causal_flash_t(range(2), load_q, load_k, load_vt, score_mod, lambda hh, queries: None, emit)


def _moba_attention(qkv, table):
    B = qkv.shape[0]
    nblk = HALF // LANES
    return pl.pallas_call(
        _moba_kernel,
        grid=(N_HEADS_C // 2, B),
        in_specs=[_smem_spec(), _head_spec(0), _head_spec(nblk), _head_spec(2 * nblk)],
        out_specs=_head_spec(0),
        out_shape=jax.ShapeDtypeStruct((B, SEQ, HALF), bf16),
        scratch_shapes=[pltpu.VMEM((2, 2 * ATT_TILE, ATT_TILE), f32),
                        pltpu.VMEM((2 * (HEAD_DIM + SUM_ROWS), SEQ), bf16)],
        compiler_params=_attention_params(),
        name="moba_attention",
    )(table, qkv, qkv, qkv)


def _forget_cum_kernel(fd_ref, fb_ref, col_ref, row_ref):
    x = jax.nn.log_sigmoid(fd_ref[0] + fb_ref[...])
    t = lax.broadcasted_iota(jnp.int32, (SEQ, LANES), 0)
    shift = 1
    while shift < SEQ:
        x = x + jnp.where(t >= shift, pltpu.roll(x, shift, axis=0), 0.0)
        shift *= 2
    col_ref[0] = x
    for c in range(SEQ // LANES):
        blk = x[c * LANES:(c + 1) * LANES, :].T
        row_ref[0, :, c * LANES:(c + 1) * LANES] = blk[0:N_HEADS_D, :]


def _forget_cum(fd, forget_b):
    B = fd.shape[0]
    fb = jnp.zeros((1, LANES), f32).at[0, :N_HEADS_D].set(forget_b)
    return pl.pallas_call(
        _forget_cum_kernel,
        grid=(B,),
        in_specs=[pl.BlockSpec((1, SEQ, LANES), lambda b: (b, 0, 0)),
                  pl.BlockSpec((1, LANES), lambda b: (0, 0))],
        out_specs=[pl.BlockSpec((1, SEQ, LANES), lambda b: (b, 0, 0)),
                   pl.BlockSpec((1, N_HEADS_D, SEQ), lambda b: (b, 0, 0))],
        out_shape=[jax.ShapeDtypeStruct((B, SEQ, LANES), f32),
                   jax.ShapeDtypeStruct((B, N_HEADS_D, SEQ), f32)],
        compiler_params=pltpu.CompilerParams(
            dimension_semantics=("arbitrary",), vmem_limit_bytes=VMEM_LIMIT),
        name="forget_cum",
    )(fd, fb)


def _fox_kernel(ccol_ref, crow_ref, q_ref, k_ref, v_ref, o_ref, col_sc, vt_sc):
    hp = pl.program_id(0)
    lane = lax.broadcasted_iota(jnp.int32, (SEQ, LANES), 1)
    for hh in range(2):
        col_sc[hh] = jnp.sum(jnp.where(lane == 2 * hp + hh, ccol_ref[0], 0.0), axis=1,
                             keepdims=True) * LOG2_E
    _store_transposed(v_ref, vt_sc, HEAD_DIM)

    def load_q(hh, queries):
        return q_ref[0, queries, hh * HEAD_DIM:(hh + 1) * HEAD_DIM]

    def load_k(hh, keys):
        return k_ref[0, keys, hh * HEAD_DIM:(hh + 1) * HEAD_DIM]

    def load_vt(hh, keys):
        return vt_sc[_vt_rows(hh, HEAD_DIM), keys]

    def score_mod(hh, queries, keys, t):
        return t - col_sc[hh, keys, :]

    def query_shift(hh, queries):
        return crow_ref[0, pl.ds(2 * hp + hh, 1), queries] * LOG2_E

    def emit(queries, results):
        o_t = jnp.concatenate([acc / l for acc, l in results], axis=0)
        o_ref[0, queries, :] = o_t.T.astype(bf16)

    _causal_flash_t(range(2), load_q, load_k, load_vt, score_mod, query_shift, emit)


def _fox_attention(qkv, cum_col, cum_row):
    B = qkv.shape[0]
    nblk = HALF // LANES
    first = 3 * nblk
    return pl.pallas_call(
        _fox_kernel,
        grid=(N_HEADS_D // 2, B),
        in_specs=[pl.BlockSpec((1, SEQ, LANES), lambda g, b: (b, 0, 0)),
                  pl.BlockSpec((1, N_HEADS_D, SEQ), lambda g, b: (b, 0, 0)),
                  _head_spec(first), _head_spec(first + nblk), _head_spec(first + 2 * nblk)],
        out_specs=_head_spec(0),
        out_shape=jax.ShapeDtypeStruct((B, SEQ, HALF), bf16),
        scratch_shapes=[pltpu.VMEM((2, SEQ, 1), f32), pltpu.VMEM((2 * (HEAD_DIM + SUM_ROWS), SEQ), bf16)],
        compiler_params=_attention_params(),
        name="fox_attention",
    )(cum_col, cum_row, qkv, qkv, qkv)


def _post_kernel(o1_ref, o2_ref, x_ref, g1_ref, sc2_ref, sh2_ref, g2_ref, ln_ref,
                 wo_ref, wi_ref, wout_ref, out_ref):
    ln = ln_ref[...]
    n_sub = POST_ROW_TILE // ROW_SUB

    def mix(r):
        rows = slice(r * ROW_SUB, (r + 1) * ROW_SUB)
        y = _dot(o1_ref[0, rows, :], wo_ref[0:HALF, :]) + _dot(o2_ref[0, rows, :], wo_ref[HALF:, :])
        x1 = _layer_norm(DEEPNORM_ALPHA * x_ref[0, rows, :] + (1.0 + g1_ref[0]) * y, ln[0:1], ln[1:2])
        return x1, (x1 * (1.0 + sc2_ref[0]) + sh2_ref[0]).astype(bf16)

    def ffn_chunk(h, chunk):
        c0, c1 = chunk
        g = _dot(h, wi_ref[:, c0:c1])
        u = _dot(h, wi_ref[:, D_FF + c0:D_FF + c1])
        return _dot((g * jax.nn.sigmoid(g) * u).astype(bf16), wout_ref[c0:c1, :])

    def finish(r, x1, y2):
        rows = slice(r * ROW_SUB, (r + 1) * ROW_SUB)
        out_ref[0, rows, :] = _layer_norm(DEEPNORM_ALPHA * x1 + (1.0 + g2_ref[0]) * y2,
                                          ln[2:3], ln[3:4])

    staged, done = mix(0), None
    for r in range(n_sub):
        x1, h = staged
        y2 = ffn_chunk(h, FF_CHUNKS[0])
        if done is not None:
            finish(*done)
        if r + 1 < n_sub:
            staged = mix(r + 1)
        done = (r, x1, y2 + ffn_chunk(h, FF_CHUNKS[1]))
    finish(*done)


def _post(o1, o2, x, g1, sc2, sh2, g2, ln, w_o, w_ffn_in, w_ffn_out):
    B = x.shape[0]
    return pl.pallas_call(
        _post_kernel,
        grid=(B, SEQ // POST_ROW_TILE),
        in_specs=[_row_spec(HALF, POST_ROW_TILE), _row_spec(HALF, POST_ROW_TILE),
                  _row_spec(D_MODEL, POST_ROW_TILE),
                  _batch_vec_spec(), _batch_vec_spec(), _batch_vec_spec(), _batch_vec_spec(),
                  pl.BlockSpec((4, D_MODEL), lambda b, i: (0, 0)),
                  _resident(w_o.shape), _resident(w_ffn_in.shape), _resident(w_ffn_out.shape)],
        out_specs=_row_spec(D_MODEL, POST_ROW_TILE),
        out_shape=jax.ShapeDtypeStruct((B, SEQ, D_MODEL), f32),
        compiler_params=pltpu.CompilerParams(
            dimension_semantics=("arbitrary", "arbitrary"), vmem_limit_bytes=VMEM_LIMIT),
        name="out_proj_ffn",
    )(o1, o2, x, g1, sc2, sh2, g2, ln, w_o, w_ffn_in, w_ffn_out)


def _q_column_scale():
    scale = np.ones((1, IN_MAIN), np.float32)
    scale[0, 0:HALF] = ATTN_SCALE * LOG2_E
    scale[0, 3 * HALF:4 * HALF] = ATTN_SCALE * LOG2_E
    return jnp.asarray(scale)


def kernel(x, c, rel_bias, w_ada, b_ada, ln_g, ln_b, w_in_ab, diff_lambda, diff_subln_g,
           w_in_cd, forget_b, w_o, w_ffn_in, w_ffn_out):
    B = x.shape[0]
    ada = _ada(c, w_ada, b_ada)
    for l in range(DEPTH):
        sh1, sc1, g1, sh2, sc2, g2 = [a.reshape(B, 1, D_MODEL) for a in jnp.split(ada[l], 6, axis=-1)]
        i = l // 2
        if l % 2 == 0:
            lambda_init = 0.8 - 0.6 * math.exp(-0.3 * l)
            qkv = _in_proj(x, sc1, sh1, w_in_ab[i].astype(bf16))
            o1 = _diff_attention(qkv, rel_bias, diff_lambda[i], diff_subln_g[i], lambda_init)
            o2 = _dilated_attention(qkv, rel_bias)
        else:
            w_gate = jnp.zeros((D_MODEL, LANES), f32).at[:, :N_HEADS_D].set(w_in_cd[i][:, IN_MAIN:])
            qkv, fd = _in_proj(x, sc1, sh1, w_in_cd[i][:, :IN_MAIN].astype(bf16), w_gate.astype(bf16))
            cum_col, cum_row = _forget_cum(fd, forget_b[i])
            o1 = _moba_attention(qkv, rel_bias)
            o2 = _fox_attention(qkv, cum_col, cum_row)
        ln = jnp.stack([ln_g[l, 0], ln_b[l, 0], ln_g[l, 1], ln_b[l, 1]])
        x = _post(o1, o2, x, g1, sc2, sh2, g2, ln, w_o[l].astype(bf16),
                  w_ffn_in[l].astype(bf16), w_ffn_out[l].astype(bf16))
    return x
```

```python
import functools
import math

import numpy as np
import jax
import jax.numpy as jnp
from jax import lax
from jax.experimental import pallas as pl
from jax.experimental.pallas import tpu as pltpu

f32 = jnp.float32
bf16 = jnp.bfloat16

D_MODEL = 1024
SEQ = 2048
DEPTH = 2
HEAD_DIM = 64
N_HEADS_A = 4
DV_A = 2 * HEAD_DIM
N_HEADS_B = 8
DILATED_PATTERNS = ((128, 1), (512, 4), (2048, 16))
DIL_BLOCK = 128
N_HEADS_C = 8
MOBA_BLOCK = 256
MOBA_TOPK = 3
N_MOBA_BLOCKS = SEQ // MOBA_BLOCK
N_HEADS_D = 8
NUM_BUCKETS = 32
MAX_DISTANCE = 128
D_FF = 2816
LN_EPS = 1e-5
DEEPNORM_ALPHA = (2 * DEPTH) ** 0.25
ATTN_SCALE = HEAD_DIM ** -0.5
LOG2_E = math.log2(math.e)
HALF = D_MODEL // 2
IN_MAIN = 3 * D_MODEL

LANES = 128
ATT_TILE = 256
KEY_CHUNK = 256
N_ATT_TILES = SEQ // ATT_TILE
SUM_ROWS = 16
PIPELINE_DEPTH = 6
MXU_DIM = 256
FF_CHUNKS = ((0, 6 * MXU_DIM), (6 * MXU_DIM, D_FF))
ROW_TILE = 1024
POST_ROW_TILE = 1024
ROW_SUB = 256
VMEM_LIMIT = 56 * 1024 * 1024

NEG_INF = float("-inf")


def _t5_bucket_starts():
    n = np.arange(0, 4 * SEQ, dtype=np.int64)
    max_exact = NUM_BUCKETS // 2
    nf = np.maximum(n, 1).astype(np.float32)
    val = (np.log(nf / np.float32(max_exact)) / np.float32(math.log(MAX_DISTANCE / max_exact))
           * np.float32(NUM_BUCKETS - max_exact))
    large = np.minimum(max_exact + val.astype(np.int32), NUM_BUCKETS - 1)
    bucket = np.where(n < max_exact, n, large)
    assert np.all(np.diff(bucket) >= 0)
    return tuple(int(np.argmax(bucket >= b)) for b in range(NUM_BUCKETS))


T5_STARTS = _t5_bucket_starts()
assert T5_STARTS[-1] <= ATT_TILE + 1


def _t5_bias(dist, tbl_ref, col):
    out = jnp.full(dist.shape, tbl_ref[0, col], f32)
    for b in range(1, NUM_BUCKETS):
        out = jnp.where(dist >= T5_STARTS[b], tbl_ref[b, col], out)
    return out


def _dot_nt(a, b):
    return lax.dot_general(a, b, (((1,), (1,)), ((), ())), preferred_element_type=f32)


def _dot(a, b):
    return jnp.dot(a, b, preferred_element_type=f32)


def _software_pipeline(items, first_stage, second_stage):
    pending = [first_stage(it) for it in items[:PIPELINE_DEPTH]]
    for idx, it in enumerate(items):
        cur = pending.pop(0)
        if idx + PIPELINE_DEPTH < len(items):
            pending.append(first_stage(items[idx + PIPELINE_DEPTH]))
        second_stage(it, cur)


def _store_transposed(v_ref, vt_ref, group):
    ones = jnp.ones((SUM_ROWS, ATT_TILE), bf16)
    for c in range(N_ATT_TILES):
        rows = slice(c * ATT_TILE, (c + 1) * ATT_TILE)
        vt = v_ref[0, rows, :].astype(f32).T.astype(bf16)
        for g in range(LANES // group):
            base = g * (group + SUM_ROWS)
            vt_ref[base:base + group, rows] = vt[g * group:(g + 1) * group, :]
            vt_ref[base + group:base + group + SUM_ROWS, rows] = ones


def _vt_rows(g, group):
    return slice(g * (group + SUM_ROWS), (g + 1) * (group + SUM_ROWS))


def _causal_flash_t(streams, load_q, load_k, load_vt, score_mod, query_shift, emit):
    streams = list(streams)
    krow = lax.broadcasted_iota(jnp.int32, (KEY_CHUNK, ATT_TILE), 0)
    qcol = lax.broadcasted_iota(jnp.int32, (KEY_CHUNK, ATT_TILE), 1)
    per_tile = ATT_TILE // KEY_CHUNK
    state = {}

    def spans(work):
        _, i, c = work
        return (slice(i * ATT_TILE, (i + 1) * ATT_TILE), slice(c * KEY_CHUNK, (c + 1) * KEY_CHUNK))

    def first_stage(work):
        s, i, c = work
        queries, keys = spans(work)
        t = score_mod(s, queries, keys, _dot_nt(load_k(s, keys), load_q(s, queries)))
        if keys.stop - 1 > queries.start:
            t = jnp.where(krow + (keys.start - queries.start) <= qcol, t, NEG_INF)
        return t, jnp.max(t, axis=0, keepdims=True)

    def second_stage(work, staged):
        s, i, c = work
        queries, keys = spans(work)
        t, tmax = staged
        m, l, acc = state.get((s, i), (None, None, None))
        m_new = tmax if m is None else jnp.maximum(m, tmax)
        shift = query_shift(s, queries)
        if shift is None:
            p = jnp.exp2(t - m_new)
        else:
            p = jnp.exp2(t + (shift - (m_new + shift)))
        pv = _dot(load_vt(s, keys), p.astype(bf16))
        dv = pv.shape[0] - SUM_ROWS
        pv, psum = pv[:dv], pv[dv:dv + 1]
        if m is None:
            l, acc = psum, pv
        else:
            alpha = jnp.exp2(m - m_new)
            l, acc = alpha * l + psum, alpha * acc + pv
        state[(s, i)] = (m_new, l, acc)
        if (s, c) == last_of_tile[i]:
            emit(queries, [(state[(s2, i)][2], state[(s2, i)][1]) for s2 in streams])

    work, last_of_tile = [], {}
    for i in reversed(range(N_ATT_TILES)):
        first = i * per_tile
        order = [first] + [c for c in range((i + 1) * per_tile) if c != first]
        work += [(s, i, c) for c in order for s in streams]
        last_of_tile[i] = (streams[-1], order[-1])
    _software_pipeline(work, first_stage, second_stage)


def _layer_norm(x, g, b):
    mu = jnp.mean(x, axis=1, keepdims=True)
    xc = x - mu
    var = jnp.mean(xc * xc, axis=1, keepdims=True)
    return xc * lax.rsqrt(var + LN_EPS) * g + b


def _ada_kernel(c_ref, w_ref, b_ref, o_ref):
    c = c_ref[...]
    cs = c * jax.nn.sigmoid(c)
    o_ref[0] = _dot(cs.astype(bf16), w_ref[0].astype(bf16)) + b_ref[0]


def _ada(c, w_ada, b_ada):
    B = c.shape[0]
    tn = 1536
    return pl.pallas_call(
        _ada_kernel,
        grid=(DEPTH, 6 * D_MODEL // tn),
        in_specs=[
            pl.BlockSpec((B, D_MODEL), lambda l, j: (0, 0)),
            pl.BlockSpec((1, D_MODEL, tn), lambda l, j: (l, 0, j)),
            pl.BlockSpec((1, 1, tn), lambda l, j: (l, 0, j)),
        ],
        out_specs=pl.BlockSpec((1, B, tn), lambda l, j: (l, 0, j)),
        out_shape=jax.ShapeDtypeStruct((DEPTH, B, 6 * D_MODEL), f32),
        compiler_params=pltpu.CompilerParams(
            dimension_semantics=("arbitrary", "arbitrary"), vmem_limit_bytes=VMEM_LIMIT),
        name="ada",
    )(c, w_ada, b_ada.reshape(DEPTH, 1, 6 * D_MODEL))


def _proj_kernel(x_ref, sc_ref, sh_ref, w_ref, qs_ref, o_ref):
    h = (x_ref[0] * (1.0 + sc_ref[0]) + sh_ref[0]).astype(bf16)
    o_ref[0] = (_dot(h, w_ref[...]) * qs_ref[...]).astype(bf16)


def _proj_gate_kernel(x_ref, sc_ref, sh_ref, w_ref, qs_ref, wf_ref, o_ref, f_ref):
    h = (x_ref[0] * (1.0 + sc_ref[0]) + sh_ref[0]).astype(bf16)
    o_ref[0] = (_dot(h, w_ref[...]) * qs_ref[...]).astype(bf16)
    f_ref[0] = _dot(h, wf_ref[...])


def _resident(shape):
    zeros = (0,) * len(shape)
    return pl.BlockSpec(shape, lambda *_: zeros, pipeline_mode=pl.Buffered(1))


def _resident_layer(stacked_shape, index, cols=None):
    rows, width = stacked_shape[1:]
    return pl.BlockSpec((None, rows, width if cols is None else cols), lambda *_: (index, 0, 0),
                        pipeline_mode=pl.Buffered(1))


def _row_spec(width, rows=ROW_TILE):
    return pl.BlockSpec((1, rows, width), lambda b, i: (b, i, 0))


def _batch_vec_spec():
    return pl.BlockSpec((1, 1, D_MODEL), lambda b, i: (b, 0, 0))


def _in_proj(x, sc, sh, w, index, w_gate=None):
    B = x.shape[0]
    in_specs = [_row_spec(D_MODEL), _batch_vec_spec(), _batch_vec_spec(),
                _resident_layer(w.shape, index, IN_MAIN), _resident((1, IN_MAIN))]
    out_specs = [_row_spec(IN_MAIN)]
    out_shape = [jax.ShapeDtypeStruct((B, SEQ, IN_MAIN), bf16)]
    args = [x, sc, sh, w, _q_column_scale()]
    kern = _proj_kernel
    if w_gate is not None:
        in_specs.append(_resident(w_gate.shape))
        out_specs.append(_row_spec(LANES))
        out_shape.append(jax.ShapeDtypeStruct((B, SEQ, LANES), f32))
        args.append(w_gate)
        kern = _proj_gate_kernel
    res = pl.pallas_call(
        kern,
        grid=(B, SEQ // ROW_TILE),
        in_specs=in_specs,
        out_specs=out_specs,
        out_shape=out_shape,
        compiler_params=pltpu.CompilerParams(
            dimension_semantics=("arbitrary", "arbitrary"), vmem_limit_bytes=VMEM_LIMIT),
        name="in_proj",
    )(*args)
    return res if w_gate is not None else res[0]


def _near_bias_t(tbl_ref, head):
    kk = lax.broadcasted_iota(jnp.int32, (2 * ATT_TILE, ATT_TILE), 0)
    qq = lax.broadcasted_iota(jnp.int32, (2 * ATT_TILE, ATT_TILE), 1)
    return (_t5_bias(qq - kk + ATT_TILE, tbl_ref, head) - tbl_ref[NUM_BUCKETS - 1, head]) * LOG2_E


def _add_near_bias(t, bias_ref, lead, queries, keys):
    rel = keys.start - (queries.start - ATT_TILE)
    if rel < 0:
        return t
    return t + bias_ref[lead + (slice(rel, rel + KEY_CHUNK), slice(None))]


def _attention_params():
    return pltpu.CompilerParams(
        dimension_semantics=("arbitrary", "arbitrary"), vmem_limit_bytes=VMEM_LIMIT)


def _smem_spec():
    return pl.BlockSpec(memory_space=pltpu.SMEM)


def _head_spec(first_block):
    return pl.BlockSpec((1, SEQ, LANES), lambda g, b: (b, 0, first_block + g))


def _diff_kernel(tbl_ref, lam_ref, g_ref, q_ref, k_ref, v_ref, o_ref, bias_sc, vt_sc, *,
                 lambda_init):
    h = pl.program_id(0)

    @pl.when(pl.program_id(1) == 0)
    def _():
        bias_sc[...] = _near_bias_t(tbl_ref, h)

    _store_transposed(v_ref, vt_sc, DV_A)
    lm = lam_ref[...]
    lam_full = (jnp.exp(jnp.sum(lm[0:1] * lm[1:2], axis=1, keepdims=True))
                - jnp.exp(jnp.sum(lm[2:3] * lm[3:4], axis=1, keepdims=True)) + lambda_init)

    def load_q(m, queries):
        return q_ref[0, queries, m * HEAD_DIM:(m + 1) * HEAD_DIM]

    def load_k(m, keys):
        return k_ref[0, keys, m * HEAD_DIM:(m + 1) * HEAD_DIM]

    def load_vt(m, keys):
        return vt_sc[:, keys]

    def score_mod(m, queries, keys, t):
        return _add_near_bias(t, bias_sc, (), queries, keys)

    def emit(queries, results):
        (acc1, l1), (acc2, l2) = results
        o = (acc1 / l1 - lam_full * (acc2 / l2)).T
        o = o * lax.rsqrt(jnp.mean(o * o, axis=1, keepdims=True) + LN_EPS)
        o_ref[0, queries, :] = (o * g_ref[...] * (1.0 - lambda_init)).astype(bf16)

    _causal_flash_t(range(2), load_q, load_k, load_vt, score_mod, lambda m, queries: None, emit)


def _diff_attention(qkv, table, lam, subln_g, lambda_init):
    B = qkv.shape[0]
    nblk = HALF // LANES
    return pl.pallas_call(
        functools.partial(_diff_kernel, lambda_init=lambda_init),
        grid=(N_HEADS_A, B),
        in_specs=[
            _smem_spec(),
            pl.BlockSpec((4, HEAD_DIM), lambda g, b: (0, 0)),
            pl.BlockSpec((1, DV_A), lambda g, b: (0, 0)),
            _head_spec(0), _head_spec(nblk), _head_spec(2 * nblk),
        ],
        out_specs=_head_spec(0),
        out_shape=jax.ShapeDtypeStruct((B, SEQ, HALF), bf16),
        scratch_shapes=[pltpu.VMEM((2 * ATT_TILE, ATT_TILE), f32),
                        pltpu.VMEM((DV_A + SUM_ROWS, SEQ), bf16)],
        compiler_params=_attention_params(),
        name="diff_attention",
    )(table, lam, subln_g.reshape(1, DV_A), qkv, qkv, qkv)


def _dil_kernel(tbl_ref, q_ref, k_ref, v_ref, o_ref, qf, kf, vf, bias_sc, osc, msc, dsc):
    hp = pl.program_id(0)
    W = DIL_BLOCK
    npat = len(DILATED_PATTERNS)
    row = lax.broadcasted_iota(jnp.int32, (W, 2 * W), 0)
    col = lax.broadcasted_iota(jnp.int32, (W, 2 * W), 1)
    du = row - col + W

    @pl.when(pl.program_id(1) == 0)
    def _():
        for p, (_, dil) in enumerate(DILATED_PATTERNS):
            for hh in range(2):
                bias = _t5_bias(du * dil, tbl_ref, N_HEADS_A + 2 * hp + hh)
                bias_sc[p, hh] = jnp.where((du >= 0) & (du <= W), bias * LOG2_E, NEG_INF)

    qf[...] = q_ref[0].astype(f32)
    kf[...] = k_ref[0].astype(f32)
    vf[...] = v_ref[0].astype(f32)
    lane = lax.broadcasted_iota(jnp.int32, (W, LANES), 1)
    head_lanes = [lane < HEAD_DIM, lane >= HEAD_DIM]
    blocks, staged_out = {}, {}

    def block(p, r, n):
        if (p, r, n) not in blocks:
            dil = DILATED_PATTERNS[p][1]
            rows = pl.ds(r + n * W * dil, W, stride=dil)
            blocks[(p, r, n)] = tuple(ref[rows, :].astype(bf16) for ref in (qf, kf, vf))
        return blocks[(p, r, n)]

    def scores(item):
        p, r, n, hh = item
        q, k, _ = block(p, r, n)
        q = jnp.where(head_lanes[hh], q, jnp.zeros_like(q))
        if n > 0:
            k = jnp.concatenate([block(p, r, n - 1)[1], k], axis=0)
            s = _dot_nt(q, k) + bias_sc[p, hh]
        else:
            s = _dot_nt(q, k) + bias_sc[p, hh, :, W:]
        return s, jnp.max(s, axis=1, keepdims=True)

    def finish(item, staged):
        p, r, n, hh = item
        s, m = staged
        v = block(p, r, n)[2]
        if n > 0:
            v = jnp.concatenate([block(p, r, n - 1)[2], v], axis=0)
        e = jnp.exp2(s - m)
        num = _dot(e.astype(bf16), jnp.concatenate([v, jnp.ones_like(v)], axis=1))
        staged_out[hh] = (num[:, :LANES], m, num[:, LANES:])
        if hh == 1:
            dil = DILATED_PATTERNS[p][1]
            rows = pl.ds(r + n * W * dil, W, stride=dil)
            for ref, a0, a1 in zip((osc, msc, dsc), staged_out[0], staged_out[1]):
                ref[p, rows, :] = jnp.where(head_lanes[0], a0, a1)

    items = []
    for p, (window, dil) in enumerate(DILATED_PATTERNS):
        assert window // dil == W
        nb = SEQ // dil // W
        items += [(p, r, n, hh) for r in range(dil) for n in range(nb) for hh in range(2)]
    _software_pipeline(items, scores, finish)

    def merge_body(i, _):
        rows = pl.ds(pl.multiple_of(i * ATT_TILE, ATT_TILE), ATT_TILE)
        ms = [msc[p, rows, :] for p in range(npat)]
        mx = functools.reduce(jnp.maximum, ms)
        ws = [jnp.exp2(m - mx) for m in ms]
        add = lambda a, b: a + b
        numer = functools.reduce(add, [w * osc[p, rows, :] for p, w in enumerate(ws)])
        denom = functools.reduce(add, [w * dsc[p, rows, :] for p, w in enumerate(ws)])
        o_ref[0, rows, :] = (numer / denom).astype(bf16)
        return 0

    lax.fori_loop(0, N_ATT_TILES, merge_body, 0)


def _dilated_attention(qkv, table):
    B = qkv.shape[0]
    nblk = HALF // LANES
    first = 3 * nblk
    npat = len(DILATED_PATTERNS)
    return pl.pallas_call(
        _dil_kernel,
        grid=(N_HEADS_B // 2, B),
        in_specs=[_smem_spec(), _head_spec(first), _head_spec(first + nblk),
                  _head_spec(first + 2 * nblk)],
        out_specs=_head_spec(0),
        out_shape=jax.ShapeDtypeStruct((B, SEQ, HALF), bf16),
        scratch_shapes=[
            pltpu.VMEM((SEQ, LANES), f32), pltpu.VMEM((SEQ, LANES), f32), pltpu.VMEM((SEQ, LANES), f32),
            pltpu.VMEM((npat, 2, DIL_BLOCK, 2 * DIL_BLOCK), f32),
            pltpu.VMEM((npat, SEQ, LANES), f32), pltpu.VMEM((npat, SEQ, LANES), f32),
            pltpu.VMEM((npat, SEQ, LANES), f32),
        ],
        compiler_params=_attention_params(),
        name="dilated_attention",
    )(table, qkv, qkv, qkv)


def _moba_kernel(tbl_ref, q_ref, k_ref, v_ref, o_ref, bias_sc, vt_sc):
    hp = pl.program_id(0)
    assert MOBA_BLOCK == ATT_TILE and MOBA_BLOCK % KEY_CHUNK == 0

    @pl.when(pl.program_id(1) == 0)
    def _():
        for hh in range(2):
            bias_sc[hh] = _near_bias_t(tbl_ref, 2 * hp + hh)

    _store_transposed(v_ref, vt_sc, HEAD_DIM)

    def load_q(hh, queries):
        return q_ref[0, queries, hh * HEAD_DIM:(hh + 1) * HEAD_DIM]

    def load_k(hh, keys):
        return k_ref[0, keys, hh * HEAD_DIM:(hh + 1) * HEAD_DIM]

    def load_vt(hh, keys):
        return vt_sc[_vt_rows(hh, HEAD_DIM), keys]

    k_means = []
    for hh in range(2):
        means = [jnp.sum(load_k(hh, slice(n * MOBA_BLOCK, (n + 1) * MOBA_BLOCK)).astype(f32), axis=0,
                         keepdims=True) * (1.0 / MOBA_BLOCK) for n in range(N_MOBA_BLOCKS)]
        k_means.append(jnp.concatenate(means + [jnp.zeros_like(m) for m in means], axis=0).astype(bf16))

    block_masks = {}

    def block_mask(hh, queries, blk):
        own = queries.start // MOBA_BLOCK
        if own <= MOBA_TOPK:
            return None
        if (hh, own) not in block_masks:
            gate = _dot_nt(k_means[hh], load_q(hh, queries))
            g = [gate[n:n + 1, :] for n in range(own)]
            masks = []
            for n in range(own):
                rank = jnp.zeros((1, ATT_TILE), jnp.int32)
                for m in range(own):
                    if m != n:
                        beats = (g[m] >= g[n]) if m < n else (g[m] > g[n])
                        rank = rank + beats.astype(jnp.int32)
                masks.append(jnp.where(rank < MOBA_TOPK, 0.0, NEG_INF))
            block_masks[(hh, own)] = masks
        return block_masks[(hh, own)][blk]

    def score_mod(hh, queries, keys, t):
        t = _add_near_bias(t, bias_sc, (hh,), queries, keys)
        if keys.start < queries.start:
            mask = block_mask(hh, queries, keys.start // MOBA_BLOCK)
            if mask is not None:
                t = t + mask
        return t

    def emit(queries, results):
        o_t = jnp.concatenate([acc / l for acc, l in results], axis=0)
        o_ref[0, queries, :] = o_t.T.astype(bf16)

    _causal_flash_t(range(2), load_q, load_k, load_vt, score_mod, lambda hh, queries: None, emit)


def _moba_attention(qkv, table):
    B = qkv.shape[0]
    nblk = HALF // LANES
    return pl.pallas_call(
        _moba_kernel,
        grid=(N_HEADS_C // 2, B),
        in_specs=[_smem_spec(), _head_spec(0), _head_spec(nblk), _head_spec(2 * nblk)],
        out_specs=_head_spec(0),
        out_shape=jax.ShapeDtypeStruct((B, SEQ, HALF), bf16),
        scratch_shapes=[pltpu.VMEM((2, 2 * ATT_TILE, ATT_TILE), f32),
                        pltpu.VMEM((2 * (HEAD_DIM + SUM_ROWS), SEQ), bf16)],
        compiler_params=_attention_params(),
        name="moba_attention",
    )(table, qkv, qkv, qkv)


def _forget_cum_kernel(fd_ref, fb_ref, col_ref, row_ref):
    x = jax.nn.log_sigmoid(fd_ref[0] + fb_ref[...])
    t = lax.broadcasted_iota(jnp.int32, (SEQ, LANES), 0)
    shift = 1
    while shift < SEQ:
        x = x + jnp.where(t >= shift, pltpu.roll(x, shift, axis=0), 0.0)
        shift *= 2
    col_ref[0] = x
    for c in range(SEQ // LANES):
        blk = x[c * LANES:(c + 1) * LANES, :].T
        row_ref[0, :, c * LANES:(c + 1) * LANES] = blk[0:N_HEADS_D, :]


def _forget_cum(fd, forget_b):
    B = fd.shape[0]
    fb = jnp.zeros((1, LANES), f32).at[0, :N_HEADS_D].set(forget_b)
    return pl.pallas_call(
        _forget_cum_kernel,
        grid=(B,),
        in_specs=[pl.BlockSpec((1, SEQ, LANES), lambda b: (b, 0, 0)),
                  pl.BlockSpec((1, LANES), lambda b: (0, 0))],
        out_specs=[pl.BlockSpec((1, SEQ, LANES), lambda b: (b, 0, 0)),
                   pl.BlockSpec((1, N_HEADS_D, SEQ), lambda b: (b, 0, 0))],
        out_shape=[jax.ShapeDtypeStruct((B, SEQ, LANES), f32),
                   jax.ShapeDtypeStruct((B, N_HEADS_D, SEQ), f32)],
        compiler_params=pltpu.CompilerParams(
            dimension_semantics=("arbitrary",), vmem_limit_bytes=VMEM_LIMIT),
        name="forget_cum",
    )(fd, fb)


def _fox_kernel(ccol_ref, crow_ref, q_ref, k_ref, v_ref, o_ref, col_sc, vt_sc):
    hp = pl.program_id(0)
    lane = lax.broadcasted_iota(jnp.int32, (SEQ, LANES), 1)
    for hh in range(2):
        col_sc[hh] = jnp.sum(jnp.where(lane == 2 * hp + hh, ccol_ref[0], 0.0), axis=1,
                             keepdims=True) * LOG2_E
    _store_transposed(v_ref, vt_sc, HEAD_DIM)

    def load_q(hh, queries):
        return q_ref[0, queries, hh * HEAD_DIM:(hh + 1) * HEAD_DIM]

    def load_k(hh, keys):
        return k_ref[0, keys, hh * HEAD_DIM:(hh + 1) * HEAD_DIM]

    def load_vt(hh, keys):
        return vt_sc[_vt_rows(hh, HEAD_DIM), keys]

    def score_mod(hh, queries, keys, t):
        return t - col_sc[hh, keys, :]

    def query_shift(hh, queries):
        return crow_ref[0, pl.ds(2 * hp + hh, 1), queries] * LOG2_E

    def emit(queries, results):
        o_t = jnp.concatenate([acc / l for acc, l in results], axis=0)
        o_ref[0, queries, :] = o_t.T.astype(bf16)

    _causal_flash_t(range(2), load_q, load_k, load_vt, score_mod, query_shift, emit)


def _fox_attention(qkv, cum_col, cum_row):
    B = qkv.shape[0]
    nblk = HALF // LANES
    first = 3 * nblk
    return pl.pallas_call(
        _fox_kernel,
        grid=(N_HEADS_D // 2, B),
        in_specs=[pl.BlockSpec((1, SEQ, LANES), lambda g, b: (b, 0, 0)),
                  pl.BlockSpec((1, N_HEADS_D, SEQ), lambda g, b: (b, 0, 0)),
                  _head_spec(first), _head_spec(first + nblk), _head_spec(first + 2 * nblk)],
        out_specs=_head_spec(0),
        out_shape=jax.ShapeDtypeStruct((B, SEQ, HALF), bf16),
        scratch_shapes=[pltpu.VMEM((2, SEQ, 1), f32), pltpu.VMEM((2 * (HEAD_DIM + SUM_ROWS), SEQ), bf16)],
        compiler_params=_attention_params(),
        name="fox_attention",
    )(cum_col, cum_row, qkv, qkv, qkv)


def _post_kernel(o1_ref, o2_ref, x_ref, g1_ref, sc2_ref, sh2_ref, g2_ref, ln_ref,
                 wo_ref, wi_ref, wout_ref, out_ref):
    ln = ln_ref[...]
    n_sub = POST_ROW_TILE // ROW_SUB

    def mix(r):
        rows = slice(r * ROW_SUB, (r + 1) * ROW_SUB)
        y = _dot(o1_ref[0, rows, :], wo_ref[0:HALF, :]) + _dot(o2_ref[0, rows, :], wo_ref[HALF:, :])
        x1 = _layer_norm(DEEPNORM_ALPHA * x_ref[0, rows, :] + (1.0 + g1_ref[0]) * y, ln[0:1], ln[1:2])
        return x1, (x1 * (1.0 + sc2_ref[0]) + sh2_ref[0]).astype(bf16)

    def ffn_chunk(h, chunk):
        c0, c1 = chunk
        g = _dot(h, wi_ref[:, c0:c1])
        u = _dot(h, wi_ref[:, D_FF + c0:D_FF + c1])
        return _dot((g * jax.nn.sigmoid(g) * u).astype(bf16), wout_ref[c0:c1, :])

    def finish(r, x1, y2):
        rows = slice(r * ROW_SUB, (r + 1) * ROW_SUB)
        out_ref[0, rows, :] = _layer_norm(DEEPNORM_ALPHA * x1 + (1.0 + g2_ref[0]) * y2,
                                          ln[2:3], ln[3:4])

    staged, done = mix(0), None
    for r in range(n_sub):
        x1, h = staged
        y2 = ffn_chunk(h, FF_CHUNKS[0])
        if done is not None:
            finish(*done)
        if r + 1 < n_sub:
            staged = mix(r + 1)
        done = (r, x1, y2 + ffn_chunk(h, FF_CHUNKS[1]))
    finish(*done)


def _post(o1, o2, x, g1, sc2, sh2, g2, ln, w_o, w_ffn_in, w_ffn_out, layer):
    B = x.shape[0]
    return pl.pallas_call(
        _post_kernel,
        grid=(B, SEQ // POST_ROW_TILE),
        in_specs=[_row_spec(HALF, POST_ROW_TILE), _row_spec(HALF, POST_ROW_TILE),
                  _row_spec(D_MODEL, POST_ROW_TILE),
                  _batch_vec_spec(), _batch_vec_spec(), _batch_vec_spec(), _batch_vec_spec(),
                  pl.BlockSpec((4, D_MODEL), lambda b, i: (0, 0)),
                  _resident_layer(w_o.shape, layer), _resident_layer(w_ffn_in.shape, layer),
                  _resident_layer(w_ffn_out.shape, layer)],
        out_specs=_row_spec(D_MODEL, POST_ROW_TILE),
        out_shape=jax.ShapeDtypeStruct((B, SEQ, D_MODEL), f32),
        compiler_params=pltpu.CompilerParams(
            dimension_semantics=("arbitrary", "arbitrary"), vmem_limit_bytes=VMEM_LIMIT),
        name="out_proj_ffn",
    )(o1, o2, x, g1, sc2, sh2, g2, ln, w_o, w_ffn_in, w_ffn_out)


def _q_column_scale():
    scale = np.ones((1, IN_MAIN), np.float32)
    scale[0, 0:HALF] = ATTN_SCALE * LOG2_E
    scale[0, 3 * HALF:4 * HALF] = ATTN_SCALE * LOG2_E
    return jnp.asarray(scale)


def kernel(x, c, rel_bias, w_ada, b_ada, ln_g, ln_b, w_in_ab, diff_lambda, diff_subln_g,
           w_in_cd, forget_b, w_o, w_ffn_in, w_ffn_out):
    B = x.shape[0]
    ada = _ada(c, w_ada, b_ada)
    w_in_ab, w_in_cd, w_o, w_ffn_in, w_ffn_out = [w.astype(bf16) for w in
                                                  (w_in_ab, w_in_cd, w_o, w_ffn_in, w_ffn_out)]
    for l in range(DEPTH):
        sh1, sc1, g1, sh2, sc2, g2 = [a.reshape(B, 1, D_MODEL) for a in jnp.split(ada[l], 6, axis=-1)]
        i = l // 2
        if l % 2 == 0:
            lambda_init = 0.8 - 0.6 * math.exp(-0.3 * l)
            qkv = _in_proj(x, sc1, sh1, w_in_ab, i)
            o1 = _diff_attention(qkv, rel_bias, diff_lambda[i], diff_subln_g[i], lambda_init)
            o2 = _dilated_attention(qkv, rel_bias)
        else:
            w_gate = jnp.zeros((D_MODEL, LANES), bf16).at[:, :N_HEADS_D].set(w_in_cd[i][:, IN_MAIN:])
            qkv, fd = _in_proj(x, sc1, sh1, w_in_cd, i, w_gate)
            cum_col, cum_row = _forget_cum(fd, forget_b[i])
            o1 = _moba_attention(qkv, rel_bias)
            o2 = _fox_attention(qkv, cum_col, cum_row)
        ln = jnp.stack([ln_g[l, 0], ln_b[l, 0], ln_g[l, 1], ln_b[l, 1]])
        x = _post(o1, o2, x, g1, sc2, sh2, g2, ln, w_o, w_ffn_in, w_ffn_out, l)
    return x
```

```python
import functools
import math

import numpy as np
import jax
import jax.numpy as jnp
from jax import lax
from jax.experimental import pallas as pl
from jax.experimental.pallas import tpu as pltpu

f32 = jnp.float32
bf16 = jnp.bfloat16

D_MODEL = 1024
SEQ = 2048
DEPTH = 2
HEAD_DIM = 64
N_HEADS_A = 4
DV_A = 2 * HEAD_DIM
N_HEADS_B = 8
DILATED_PATTERNS = ((128, 1), (512, 4), (2048, 16))
DIL_BLOCK = 128
N_HEADS_C = 8
MOBA_BLOCK = 256
MOBA_TOPK = 3
N_MOBA_BLOCKS = SEQ // MOBA_BLOCK
N_HEADS_D = 8
NUM_BUCKETS = 32
MAX_DISTANCE = 128
D_FF = 2816
LN_EPS = 1e-5
DEEPNORM_ALPHA = (2 * DEPTH) ** 0.25
ATTN_SCALE = HEAD_DIM ** -0.5
LOG2_E = math.log2(math.e)
HALF = D_MODEL // 2
IN_MAIN = 3 * D_MODEL

LANES = 128
ATT_TILE = 256
KEY_CHUNK = 256
N_ATT_TILES = SEQ // ATT_TILE
SUM_ROWS = 16
PIPELINE_DEPTH = 5
MXU_DIM = 256
FF_CHUNKS = ((0, 6 * MXU_DIM), (6 * MXU_DIM, D_FF))
ROW_TILE = 1024
POST_ROW_TILE = 1024
ROW_SUB = 256
VMEM_LIMIT = 56 * 1024 * 1024

NEG_INF = float("-inf")


def _t5_bucket_starts():
    n = np.arange(0, 4 * SEQ, dtype=np.int64)
    max_exact = NUM_BUCKETS // 2
    nf = np.maximum(n, 1).astype(np.float32)
    val = (np.log(nf / np.float32(max_exact)) / np.float32(math.log(MAX_DISTANCE / max_exact))
           * np.float32(NUM_BUCKETS - max_exact))
    large = np.minimum(max_exact + val.astype(np.int32), NUM_BUCKETS - 1)
    bucket = np.where(n < max_exact, n, large)
    assert np.all(np.diff(bucket) >= 0)
    return tuple(int(np.argmax(bucket >= b)) for b in range(NUM_BUCKETS))


T5_STARTS = _t5_bucket_starts()
assert T5_STARTS[-1] <= ATT_TILE + 1


def _t5_bias(dist, tbl_ref, col):
    out = jnp.full(dist.shape, tbl_ref[0, col], f32)
    for b in range(1, NUM_BUCKETS):
        out = jnp.where(dist >= T5_STARTS[b], tbl_ref[b, col], out)
    return out


def _dot_nt(a, b):
    return lax.dot_general(a, b, (((1,), (1,)), ((), ())), preferred_element_type=f32)


def _dot(a, b):
    return jnp.dot(a, b, preferred_element_type=f32)


def _software_pipeline(items, first_stage, second_stage):
    pending = [first_stage(it) for it in items[:PIPELINE_DEPTH]]
    for idx, it in enumerate(items):
        cur = pending.pop(0)
        if idx + PIPELINE_DEPTH < len(items):
            pending.append(first_stage(items[idx + PIPELINE_DEPTH]))
        second_stage(it, cur)


def _store_transposed(v_ref, vt_ref, group):
    ones = jnp.ones((SUM_ROWS, ATT_TILE), bf16)
    for c in range(N_ATT_TILES):
        rows = slice(c * ATT_TILE, (c + 1) * ATT_TILE)
        vt = v_ref[0, rows, :].astype(f32).T.astype(bf16)
        for g in range(LANES // group):
            base = g * (group + SUM_ROWS)
            vt_ref[base:base + group, rows] = vt[g * group:(g + 1) * group, :]
            vt_ref[base + group:base + group + SUM_ROWS, rows] = ones


def _vt_rows(g, group):
    return slice(g * (group + SUM_ROWS), (g + 1) * (group + SUM_ROWS))


def _causal_flash_t(streams, load_q, load_k, load_vt, score_mod, query_shift, emit):
    streams = list(streams)
    krow = lax.broadcasted_iota(jnp.int32, (KEY_CHUNK, ATT_TILE), 0)
    qcol = lax.broadcasted_iota(jnp.int32, (KEY_CHUNK, ATT_TILE), 1)
    per_tile = ATT_TILE // KEY_CHUNK
    state = {}

    def spans(work):
        _, i, c = work
        return (slice(i * ATT_TILE, (i + 1) * ATT_TILE), slice(c * KEY_CHUNK, (c + 1) * KEY_CHUNK))

    def first_stage(work):
        s, i, c = work
        queries, keys = spans(work)
        t = score_mod(s, queries, keys, _dot_nt(load_k(s, keys), load_q(s, queries)))
        if keys.stop - 1 > queries.start:
            t = jnp.where(krow + (keys.start - queries.start) <= qcol, t, NEG_INF)
        return t, jnp.max(t, axis=0, keepdims=True)

    def second_stage(work, staged):
        s, i, c = work
        queries, keys = spans(work)
        t, tmax = staged
        m, l, acc = state.get((s, i), (None, None, None))
        m_new = tmax if m is None else jnp.maximum(m, tmax)
        shift = query_shift(s, queries)
        if shift is None:
            p = jnp.exp2(t - m_new)
        else:
            p = jnp.exp2(t + (shift - (m_new + shift)))
        pv = _dot(load_vt(s, keys), p.astype(bf16))
        dv = pv.shape[0] - SUM_ROWS
        pv, psum = pv[:dv], pv[dv:dv + 1]
        if m is None:
            l, acc = psum, pv
        else:
            alpha = jnp.exp2(m - m_new)
            l, acc = alpha * l + psum, alpha * acc + pv
        state[(s, i)] = (m_new, l, acc)
        if (s, c) == last_of_tile[i]:
            emit(queries, [(state[(s2, i)][2], state[(s2, i)][1]) for s2 in streams])

    work, last_of_tile = [], {}
    for i in reversed(range(N_ATT_TILES)):
        first = i * per_tile
        order = [first] + [c for c in range((i + 1) * per_tile) if c != first]
        work += [(s, i, c) for c in order for s in streams]
        last_of_tile[i] = (streams[-1], order[-1])
    _software_pipeline(work, first_stage, second_stage)


def _layer_norm(x, g, b):
    mu = jnp.mean(x, axis=1, keepdims=True)
    xc = x - mu
    var = jnp.mean(xc * xc, axis=1, keepdims=True)
    return xc * lax.rsqrt(var + LN_EPS) * g + b


def _ada_kernel(c_ref, w_ref, b_ref, o_ref):
    c = c_ref[...]
    cs = c * jax.nn.sigmoid(c)
    o_ref[0] = _dot(cs.astype(bf16), w_ref[0].astype(bf16)) + b_ref[0]


def _ada(c, w_ada, b_ada):
    B = c.shape[0]
    tn = 1536
    return pl.pallas_call(
        _ada_kernel,
        grid=(DEPTH, 6 * D_MODEL // tn),
        in_specs=[
            pl.BlockSpec((B, D_MODEL), lambda l, j: (0, 0)),
            pl.BlockSpec((1, D_MODEL, tn), lambda l, j: (l, 0, j)),
            pl.BlockSpec((1, 1, tn), lambda l, j: (l, 0, j)),
        ],
        out_specs=pl.BlockSpec((1, B, tn), lambda l, j: (l, 0, j)),
        out_shape=jax.ShapeDtypeStruct((DEPTH, B, 6 * D_MODEL), f32),
        compiler_params=pltpu.CompilerParams(
            dimension_semantics=("arbitrary", "arbitrary"), vmem_limit_bytes=VMEM_LIMIT),
        name="ada",
    )(c, w_ada, b_ada.reshape(DEPTH, 1, 6 * D_MODEL))


def _proj_kernel(x_ref, sc_ref, sh_ref, w_ref, qs_ref, o_ref):
    h = (x_ref[0] * (1.0 + sc_ref[0]) + sh_ref[0]).astype(bf16)
    o_ref[0] = (_dot(h, w_ref[...]) * qs_ref[...]).astype(bf16)


def _proj_gate_kernel(x_ref, sc_ref, sh_ref, w_ref, qs_ref, wf_ref, o_ref, f_ref):
    h = (x_ref[0] * (1.0 + sc_ref[0]) + sh_ref[0]).astype(bf16)
    o_ref[0] = (_dot(h, w_ref[...]) * qs_ref[...]).astype(bf16)
    f_ref[0] = _dot(h, wf_ref[...])


def _resident(shape):
    zeros = (0,) * len(shape)
    return pl.BlockSpec(shape, lambda *_: zeros, pipeline_mode=pl.Buffered(1))


def _resident_layer(stacked_shape, index, cols=None):
    rows, width = stacked_shape[1:]
    return pl.BlockSpec((None, rows, width if cols is None else cols), lambda *_: (index, 0, 0),
                        pipeline_mode=pl.Buffered(1))


def _row_spec(width, rows=ROW_TILE):
    return pl.BlockSpec((1, rows, width), lambda b, i: (b, i, 0))


def _batch_vec_spec():
    return pl.BlockSpec((1, 1, D_MODEL), lambda b, i: (b, 0, 0))


def _in_proj(x, sc, sh, w, index, w_gate=None):
    B = x.shape[0]
    in_specs = [_row_spec(D_MODEL), _batch_vec_spec(), _batch_vec_spec(),
                _resident_layer(w.shape, index, IN_MAIN), _resident((1, IN_MAIN))]
    out_specs = [_row_spec(IN_MAIN)]
    out_shape = [jax.ShapeDtypeStruct((B, SEQ, IN_MAIN), bf16)]
    args = [x, sc, sh, w, _q_column_scale()]
    kern = _proj_kernel
    if w_gate is not None:
        in_specs.append(_resident(w_gate.shape))
        out_specs.append(_row_spec(LANES))
        out_shape.append(jax.ShapeDtypeStruct((B, SEQ, LANES), f32))
        args.append(w_gate)
        kern = _proj_gate_kernel
    res = pl.pallas_call(
        kern,
        grid=(B, SEQ // ROW_TILE),
        in_specs=in_specs,
        out_specs=out_specs,
        out_shape=out_shape,
        compiler_params=pltpu.CompilerParams(
            dimension_semantics=("arbitrary", "arbitrary"), vmem_limit_bytes=VMEM_LIMIT),
        name="in_proj",
    )(*args)
    return res if w_gate is not None else res[0]


def _near_bias_t(tbl_ref, head):
    kk = lax.broadcasted_iota(jnp.int32, (2 * ATT_TILE, ATT_TILE), 0)
    qq = lax.broadcasted_iota(jnp.int32, (2 * ATT_TILE, ATT_TILE), 1)
    return (_t5_bias(qq - kk + ATT_TILE, tbl_ref, head) - tbl_ref[NUM_BUCKETS - 1, head]) * LOG2_E


def _add_near_bias(t, bias_ref, lead, queries, keys):
    rel = keys.start - (queries.start - ATT_TILE)
    if rel < 0:
        return t
    return t + bias_ref[lead + (slice(rel, rel + KEY_CHUNK), slice(None))]


def _attention_params():
    return pltpu.CompilerParams(
        dimension_semantics=("arbitrary", "arbitrary"), vmem_limit_bytes=VMEM_LIMIT)


def _smem_spec():
    return pl.BlockSpec(memory_space=pltpu.SMEM)


def _head_spec(first_block):
    return pl.BlockSpec((1, SEQ, LANES), lambda g, b: (b, 0, first_block + g))


def _diff_kernel(tbl_ref, lam_ref, g_ref, q_ref, k_ref, v_ref, o_ref, bias_sc, vt_sc, *,
                 lambda_init):
    h = pl.program_id(0)

    @pl.when(pl.program_id(1) == 0)
    def _():
        bias_sc[...] = _near_bias_t(tbl_ref, h)

    _store_transposed(v_ref, vt_sc, DV_A)
    lm = lam_ref[...]
    lam_full = (jnp.exp(jnp.sum(lm[0:1] * lm[1:2], axis=1, keepdims=True))
                - jnp.exp(jnp.sum(lm[2:3] * lm[3:4], axis=1, keepdims=True)) + lambda_init)

    def load_q(m, queries):
        return q_ref[0, queries, m * HEAD_DIM:(m + 1) * HEAD_DIM]

    def load_k(m, keys):
        return k_ref[0, keys, m * HEAD_DIM:(m + 1) * HEAD_DIM]

    def load_vt(m, keys):
        return vt_sc[:, keys]

    def score_mod(m, queries, keys, t):
        return _add_near_bias(t, bias_sc, (), queries, keys)

    def emit(queries, results):
        (acc1, l1), (acc2, l2) = results
        o = (acc1 / l1 - lam_full * (acc2 / l2)).T
        o = o * lax.rsqrt(jnp.mean(o * o, axis=1, keepdims=True) + LN_EPS)
        o_ref[0, queries, :] = (o * g_ref[...] * (1.0 - lambda_init)).astype(bf16)

    _causal_flash_t(range(2), load_q, load_k, load_vt, score_mod, lambda m, queries: None, emit)


def _diff_attention(qkv, table, lam, subln_g, lambda_init):
    B = qkv.shape[0]
    nblk = HALF // LANES
    return pl.pallas_call(
        functools.partial(_diff_kernel, lambda_init=lambda_init),
        grid=(N_HEADS_A, B),
        in_specs=[
            _smem_spec(),
            pl.BlockSpec((4, HEAD_DIM), lambda g, b: (0, 0)),
            pl.BlockSpec((1, DV_A), lambda g, b: (0, 0)),
            _head_spec(0), _head_spec(nblk), _head_spec(2 * nblk),
        ],
        out_specs=_head_spec(0),
        out_shape=jax.ShapeDtypeStruct((B, SEQ, HALF), bf16),
        scratch_shapes=[pltpu.VMEM((2 * ATT_TILE, ATT_TILE), f32),
                        pltpu.VMEM((DV_A + SUM_ROWS, SEQ), bf16)],
        compiler_params=_attention_params(),
        name="diff_attention",
    )(table, lam, subln_g.reshape(1, DV_A), qkv, qkv, qkv)


def _dil_kernel(tbl_ref, q_ref, k_ref, v_ref, o_ref, qf, kf, vf, bias_sc, osc, msc, dsc):
    hp = pl.program_id(0)
    W = DIL_BLOCK
    npat = len(DILATED_PATTERNS)
    row = lax.broadcasted_iota(jnp.int32, (W, 2 * W), 0)
    col = lax.broadcasted_iota(jnp.int32, (W, 2 * W), 1)
    du = row - col + W

    @pl.when(pl.program_id(1) == 0)
    def _():
        for p, (_, dil) in enumerate(DILATED_PATTERNS):
            for hh in range(2):
                bias = _t5_bias(du * dil, tbl_ref, N_HEADS_A + 2 * hp + hh)
                bias_sc[p, hh] = jnp.where((du >= 0) & (du <= W), bias * LOG2_E, NEG_INF)

    qf[...] = q_ref[0].astype(f32)
    kf[...] = k_ref[0].astype(f32)
    vf[...] = v_ref[0].astype(f32)
    lane = lax.broadcasted_iota(jnp.int32, (W, LANES), 1)
    head_lanes = [lane < HEAD_DIM, lane >= HEAD_DIM]
    blocks, staged_out = {}, {}

    def block(p, r, n):
        if (p, r, n) not in blocks:
            dil = DILATED_PATTERNS[p][1]
            rows = pl.ds(r + n * W * dil, W, stride=dil)
            blocks[(p, r, n)] = tuple(ref[rows, :].astype(bf16) for ref in (qf, kf, vf))
        return blocks[(p, r, n)]

    def scores(item):
        p, r, n, hh = item
        q, k, _ = block(p, r, n)
        q = jnp.where(head_lanes[hh], q, jnp.zeros_like(q))
        if n > 0:
            k = jnp.concatenate([block(p, r, n - 1)[1], k], axis=0)
            s = _dot_nt(q, k) + bias_sc[p, hh]
        else:
            s = _dot_nt(q, k) + bias_sc[p, hh, :, W:]
        return s, jnp.max(s, axis=1, keepdims=True)

    def finish(item, staged):
        p, r, n, hh = item
        s, m = staged
        v = block(p, r, n)[2]
        if n > 0:
            v = jnp.concatenate([block(p, r, n - 1)[2], v], axis=0)
        e = jnp.exp2(s - m)
        num = _dot(e.astype(bf16), jnp.concatenate([v, jnp.ones_like(v)], axis=1))
        staged_out[hh] = (num[:, :LANES], m, num[:, LANES:])
        if hh == 1:
            dil = DILATED_PATTERNS[p][1]
            rows = pl.ds(r + n * W * dil, W, stride=dil)
            for ref, a0, a1 in zip((osc, msc, dsc), staged_out[0], staged_out[1]):
                ref[p, rows, :] = jnp.where(head_lanes[0], a0, a1)

    items = []
    for p, (window, dil) in enumerate(DILATED_PATTERNS):
        assert window // dil == W
        nb = SEQ // dil // W
        items += [(p, r, n, hh) for r in range(dil) for n in range(nb) for hh in range(2)]
    _software_pipeline(items, scores, finish)

    def merge_body(i, _):
        rows = pl.ds(pl.multiple_of(i * ATT_TILE, ATT_TILE), ATT_TILE)
        ms = [msc[p, rows, :] for p in range(npat)]
        mx = functools.reduce(jnp.maximum, ms)
        ws = [jnp.exp2(m - mx) for m in ms]
        add = lambda a, b: a + b
        numer = functools.reduce(add, [w * osc[p, rows, :] for p, w in enumerate(ws)])
        denom = functools.reduce(add, [w * dsc[p, rows, :] for p, w in enumerate(ws)])
        o_ref[0, rows, :] = (numer / denom).astype(bf16)
        return 0

    lax.fori_loop(0, N_ATT_TILES, merge_body, 0)


def _dilated_attention(qkv, table):
    B = qkv.shape[0]
    nblk = HALF // LANES
    first = 3 * nblk
    npat = len(DILATED_PATTERNS)
    return pl.pallas_call(
        _dil_kernel,
        grid=(N_HEADS_B // 2, B),
        in_specs=[_smem_spec(), _head_spec(first), _head_spec(first + nblk),
                  _head_spec(first + 2 * nblk)],
        out_specs=_head_spec(0),
        out_shape=jax.ShapeDtypeStruct((B, SEQ, HALF), bf16),
        scratch_shapes=[
            pltpu.VMEM((SEQ, LANES), f32), pltpu.VMEM((SEQ, LANES), f32), pltpu.VMEM((SEQ, LANES), f32),
            pltpu.VMEM((npat, 2, DIL_BLOCK, 2 * DIL_BLOCK), f32),
            pltpu.VMEM((npat, SEQ, LANES), f32), pltpu.VMEM((npat, SEQ, LANES), f32),
            pltpu.VMEM((npat, SEQ, LANES), f32),
        ],
        compiler_params=_attention_params(),
        name="dilated_attention",
    )(table, qkv, qkv, qkv)


def _moba_kernel(tbl_ref, q_ref, k_ref, v_ref, o_ref, bias_sc, vt_sc):
    hp = pl.program_id(0)
    assert MOBA_BLOCK == ATT_TILE and MOBA_BLOCK % KEY_CHUNK == 0

    @pl.when(pl.program_id(1) == 0)
    def _():
        for hh in range(2):
            bias_sc[hh] = _near_bias_t(tbl_ref, 2 * hp + hh)

    _store_transposed(v_ref, vt_sc, HEAD_DIM)

    def load_q(hh, queries):
        return q_ref[0, queries, hh * HEAD_DIM:(hh + 1) * HEAD_DIM]

    def load_k(hh, keys):
        return k_ref[0, keys, hh * HEAD_DIM:(hh + 1) * HEAD_DIM]

    def load_vt(hh, keys):
        return vt_sc[_vt_rows(hh, HEAD_DIM), keys]

    k_means = []
    for hh in range(2):
        means = [jnp.sum(load_k(hh, slice(n * MOBA_BLOCK, (n + 1) * MOBA_BLOCK)).astype(f32), axis=0,
                         keepdims=True) * (1.0 / MOBA_BLOCK) for n in range(N_MOBA_BLOCKS)]
        k_means.append(jnp.concatenate(means + [jnp.zeros_like(m) for m in means], axis=0).astype(bf16))

    block_masks = {}

    def block_mask(hh, queries, blk):
        own = queries.start // MOBA_BLOCK
        if own <= MOBA_TOPK:
            return None
        if (hh, own) not in block_masks:
            gate = _dot_nt(k_means[hh], load_q(hh, queries))
            g = [gate[n:n + 1, :] for n in range(own)]
            masks = []
            for n in range(own):
                rank = jnp.zeros((1, ATT_TILE), jnp.int32)
                for m in range(own):
                    if m != n:
                        beats = (g[m] >= g[n]) if m < n else (g[m] > g[n])
                        rank = rank + beats.astype(jnp.int32)
                masks.append(jnp.where(rank < MOBA_TOPK, 0.0, NEG_INF))
            block_masks[(hh, own)] = masks
        return block_masks[(hh, own)][blk]

    def score_mod(hh, queries, keys, t):
        t = _add_near_bias(t, bias_sc, (hh,), queries, keys)
        if keys.start < queries.start:
            mask = block_mask(hh, queries, keys.start // MOBA_BLOCK)
            if mask is not None:
                t = t + mask
        return t

    def emit(queries, results):
        o_t = jnp.concatenate([acc / l for acc, l in results], axis=0)
        o_ref[0, queries, :] = o_t.T.astype(bf16)

    _causal_flash_t(range(2), load_q, load_k, load_vt, score_mod, lambda hh, queries: None, emit)


def _moba_attention(qkv, table):
    B = qkv.shape[0]
    nblk = HALF // LANES
    return pl.pallas_call(
        _moba_kernel,
        grid=(N_HEADS_C // 2, B),
        in_specs=[_smem_spec(), _head_spec(0), _head_spec(nblk), _head_spec(2 * nblk)],
        out_specs=_head_spec(0),
        out_shape=jax.ShapeDtypeStruct((B, SEQ, HALF), bf16),
        scratch_shapes=[pltpu.VMEM((2, 2 * ATT_TILE, ATT_TILE), f32),
                        pltpu.VMEM((2 * (HEAD_DIM + SUM_ROWS), SEQ), bf16)],
        compiler_params=_attention_params(),
        name="moba_attention",
    )(table, qkv, qkv, qkv)


def _forget_cum_kernel(fd_ref, fb_ref, col_ref, row_ref):
    x = jax.nn.log_sigmoid(fd_ref[0] + fb_ref[...])
    t = lax.broadcasted_iota(jnp.int32, (SEQ, LANES), 0)
    shift = 1
    while shift < SEQ:
        x = x + jnp.where(t >= shift, pltpu.roll(x, shift, axis=0), 0.0)
        shift *= 2
    col_ref[0] = x
    for c in range(SEQ // LANES):
        blk = x[c * LANES:(c + 1) * LANES, :].T
        row_ref[0, :, c * LANES:(c + 1) * LANES] = blk[0:N_HEADS_D, :]


def _forget_cum(fd, forget_b):
    B = fd.shape[0]
    fb = jnp.zeros((1, LANES), f32).at[0, :N_HEADS_D].set(forget_b)
    return pl.pallas_call(
        _forget_cum_kernel,
        grid=(B,),
        in_specs=[pl.BlockSpec((1, SEQ, LANES), lambda b: (b, 0, 0)),
                  pl.BlockSpec((1, LANES), lambda b: (0, 0))],
        out_specs=[pl.BlockSpec((1, SEQ, LANES), lambda b: (b, 0, 0)),
                   pl.BlockSpec((1, N_HEADS_D, SEQ), lambda b: (b, 0, 0))],
        out_shape=[jax.ShapeDtypeStruct((B, SEQ, LANES), f32),
                   jax.ShapeDtypeStruct((B, N_HEADS_D, SEQ), f32)],
        compiler_params=pltpu.CompilerParams(
            dimension_semantics=("arbitrary",), vmem_limit_bytes=VMEM_LIMIT),
        name="forget_cum",
    )(fd, fb)


def _fox_kernel(ccol_ref, crow_ref, q_ref, k_ref, v_ref, o_ref, col_sc, vt_sc):
    hp = pl.program_id(0)
    lane = lax.broadcasted_iota(jnp.int32, (SEQ, LANES), 1)
    for hh in range(2):
        col_sc[hh] = jnp.sum(jnp.where(lane == 2 * hp + hh, ccol_ref[0], 0.0), axis=1,
                             keepdims=True) * LOG2_E
    _store_transposed(v_ref, vt_sc, HEAD_DIM)

    def load_q(hh, queries):
        return q_ref[0, queries, hh * HEAD_DIM:(hh + 1) * HEAD_DIM]

    def load_k(hh, keys):
        return k_ref[0, keys, hh * HEAD_DIM:(hh + 1) * HEAD_DIM]

    def load_vt(hh, keys):
        return vt_sc[_vt_rows(hh, HEAD_DIM), keys]

    def score_mod(hh, queries, keys, t):
        return t - col_sc[hh, keys, :]

    def query_shift(hh, queries):
        return crow_ref[0, pl.ds(2 * hp + hh, 1), queries] * LOG2_E

    def emit(queries, results):
        o_t = jnp.concatenate([acc / l for acc, l in results], axis=0)
        o_ref[0, queries, :] = o_t.T.astype(bf16)

    _causal_flash_t(range(2), load_q, load_k, load_vt, score_mod, query_shift, emit)


def _fox_attention(qkv, cum_col, cum_row):
    B = qkv.shape[0]
    nblk = HALF // LANES
    first = 3 * nblk
    return pl.pallas_call(
        _fox_kernel,
        grid=(N_HEADS_D // 2, B),
        in_specs=[pl.BlockSpec((1, SEQ, LANES), lambda g, b: (b, 0, 0)),
                  pl.BlockSpec((1, N_HEADS_D, SEQ), lambda g, b: (b, 0, 0)),
                  _head_spec(first), _head_spec(first + nblk), _head_spec(first + 2 * nblk)],
        out_specs=_head_spec(0),
        out_shape=jax.ShapeDtypeStruct((B, SEQ, HALF), bf16),
        scratch_shapes=[pltpu.VMEM((2, SEQ, 1), f32), pltpu.VMEM((2 * (HEAD_DIM + SUM_ROWS), SEQ), bf16)],
        compiler_params=_attention_params(),
        name="fox_attention",
    )(cum_col, cum_row, qkv, qkv, qkv)


def _post_kernel(o1_ref, o2_ref, x_ref, g1_ref, sc2_ref, sh2_ref, g2_ref, ln_ref,
                 wo_ref, wi_ref, wout_ref, out_ref):
    ln = ln_ref[...]
    n_sub = POST_ROW_TILE // ROW_SUB

    def mix(r):
        rows = slice(r * ROW_SUB, (r + 1) * ROW_SUB)
        y = _dot(o1_ref[0, rows, :], wo_ref[0:HALF, :]) + _dot(o2_ref[0, rows, :], wo_ref[HALF:, :])
        x1 = _layer_norm(DEEPNORM_ALPHA * x_ref[0, rows, :] + (1.0 + g1_ref[0]) * y, ln[0:1], ln[1:2])
        return x1, (x1 * (1.0 + sc2_ref[0]) + sh2_ref[0]).astype(bf16)

    def ffn_chunk(h, chunk):
        c0, c1 = chunk
        g = _dot(h, wi_ref[:, c0:c1])
        u = _dot(h, wi_ref[:, D_FF + c0:D_FF + c1])
        return _dot((g * jax.nn.sigmoid(g) * u).astype(bf16), wout_ref[c0:c1, :])

    def finish(r, x1, y2):
        rows = slice(r * ROW_SUB, (r + 1) * ROW_SUB)
        out_ref[0, rows, :] = _layer_norm(DEEPNORM_ALPHA * x1 + (1.0 + g2_ref[0]) * y2,
                                          ln[2:3], ln[3:4])

    staged, done = mix(0), None
    for r in range(n_sub):
        x1, h = staged
        y2 = ffn_chunk(h, FF_CHUNKS[0])
        if done is not None:
            finish(*done)
        if r + 1 < n_sub:
            staged = mix(r + 1)
        done = (r, x1, y2 + ffn_chunk(h, FF_CHUNKS[1]))
    finish(*done)


def _post(o1, o2, x, g1, sc2, sh2, g2, ln, w_o, w_ffn_in, w_ffn_out, layer):
    B = x.shape[0]
    return pl.pallas_call(
        _post_kernel,
        grid=(B, SEQ // POST_ROW_TILE),
        in_specs=[_row_spec(HALF, POST_ROW_TILE), _row_spec(HALF, POST_ROW_TILE),
                  _row_spec(D_MODEL, POST_ROW_TILE),
                  _batch_vec_spec(), _batch_vec_spec(), _batch_vec_spec(), _batch_vec_spec(),
                  pl.BlockSpec((4, D_MODEL), lambda b, i: (0, 0)),
                  _resident_layer(w_o.shape, layer), _resident_layer(w_ffn_in.shape, layer),
                  _resident_layer(w_ffn_out.shape, layer)],
        out_specs=_row_spec(D_MODEL, POST_ROW_TILE),
        out_shape=jax.ShapeDtypeStruct((B, SEQ, D_MODEL), f32),
        compiler_params=pltpu.CompilerParams(
            dimension_semantics=("arbitrary", "arbitrary"), vmem_limit_bytes=VMEM_LIMIT),
        name="out_proj_ffn",
    )(o1, o2, x, g1, sc2, sh2, g2, ln, w_o, w_ffn_in, w_ffn_out)


def _q_column_scale():
    scale = np.ones((1, IN_MAIN), np.float32)
    scale[0, 0:HALF] = ATTN_SCALE * LOG2_E
    scale[0, 3 * HALF:4 * HALF] = ATTN_SCALE * LOG2_E
    return jnp.asarray(scale)


def kernel(x, c, rel_bias, w_ada, b_ada, ln_g, ln_b, w_in_ab, diff_lambda, diff_subln_g,
           w_in_cd, forget_b, w_o, w_ffn_in, w_ffn_out):
    B = x.shape[0]
    ada = _ada(c, w_ada, b_ada)
    w_in_ab, w_in_cd, w_o, w_ffn_in, w_ffn_out = [w.astype(bf16) for w in
                                                  (w_in_ab, w_in_cd, w_o, w_ffn_in, w_ffn_out)]
    for l in range(DEPTH):
        sh1, sc1, g1, sh2, sc2, g2 = [a.reshape(B, 1, D_MODEL) for a in jnp.split(ada[l], 6, axis=-1)]
        i = l // 2
        if l % 2 == 0:
            lambda_init = 0.8 - 0.6 * math.exp(-0.3 * l)
            qkv = _in_proj(x, sc1, sh1, w_in_ab, i)
            o1 = _diff_attention(qkv, rel_bias, diff_lambda[i], diff_subln_g[i], lambda_init)
            o2 = _dilated_attention(qkv, rel_bias)
        else:
            w_gate = jnp.zeros((D_MODEL, LANES), bf16).at[:, :N_HEADS_D].set(w_in_cd[i][:, IN_MAIN:])
            qkv, fd = _in_proj(x, sc1, sh1, w_in_cd, i, w_gate)
            cum_col, cum_row = _forget_cum(fd, forget_b[i])
            o1 = _moba_attention(qkv, rel_bias)
            o2 = _fox_attention(qkv, cum_col, cum_row)
        ln = jnp.stack([ln_g[l, 0], ln_b[l, 0], ln_g[l, 1], ln_b[l, 1]])
        x = _post(o1, o2, x, g1, sc2, sh2, g2, ln, w_o, w_ffn_in, w_ffn_out, l)
    return x
```
